```python
import jax, jax.numpy as jnp
from jax import lax
import numpy as np

D_MODEL = 1024
BATCH = 4
SEQ = 4096
DEPTH = 2

GRID_W = 64
CTX_LEN = 256
D_SSD = 1024
HEAD_DIM = 64
N_HEADS = D_SSD // HEAD_DIM
N_GROUPS = 4
HEADS_PER_GROUP = N_HEADS // N_GROUPS
D_STATE = 128
SSD_CONV = 5
CHUNK = 128
D_CONF = 1024
CONF_KERNEL = 31
D_MIX = D_SSD + D_CONF
XBC_DIM = D_SSD + 2 * N_GROUPS * D_STATE
SSD_IN_COLS = XBC_DIM + 2 * N_HEADS
Z_END = SSD_IN_COLS + D_SSD
IN_COLS = Z_END + 2 * D_CONF
N_EXPERTS = 16
CAPACITY_FACTOR = 2
D_EXPERT = 1408
EPS = 1e-6

kernel_name = 'hymba_ssd_conformer_ec_moe_dit'


def rms_norm(x, g):
    xf = x.astype(jnp.float32)
    y = xf * lax.rsqrt(jnp.mean(xf * xf, axis=-1, keepdims=True) + EPS)
    return (y * g.astype(jnp.float32)).astype(x.dtype)


def layer_norm(x, g, b):
    xf = x.astype(jnp.float32)
    xc = xf - jnp.mean(xf, axis=-1, keepdims=True)
    var = jnp.mean(xc * xc, axis=-1, keepdims=True)
    y = xc * lax.rsqrt(var + EPS) * g.astype(jnp.float32) + b.astype(jnp.float32)
    return y.astype(x.dtype)


def depthwise_conv(u, w, bias):
    k = w.shape[0]
    out = lax.conv_general_dilated(u, w[:, None, :].astype(u.dtype), window_strides=(1,),
                                   padding=[(k // 2, k // 2)],
                                   dimension_numbers=('NWC', 'WIO', 'NWC'),
                                   feature_group_count=u.shape[-1])
    return out + bias.astype(u.dtype)


def adaln(cond, ada_w, ada_b):
    return jnp.split(jax.nn.silu(cond) @ ada_w + ada_b, 6, axis=-1)


def _rev(t):
    return jnp.flip(t, axis=1)


def _ssd_prep(x, dt, a, bm):
    b, l = x.shape[:2]
    nc = l // CHUNK
    xdt = (x.astype(jnp.float32) * dt[..., None]).reshape(b, nc, CHUNK, N_GROUPS, HEADS_PER_GROUP, HEAD_DIM)
    bc = bm.astype(jnp.float32).reshape(b, nc, CHUNK, N_GROUPS, D_STATE)
    cum = jnp.cumsum((dt * a).reshape(b, nc, CHUNK, N_GROUPS, HEADS_PER_GROUP), axis=2)
    return xdt, bc, cum


def _ssd_inter_chunk(xdt, bc, cum, h0):
    to_end = jnp.exp(cum[:, :, -1:] - cum)
    states = jnp.einsum('bclgn,bclgr,bclgrp->bcgrpn', bc, to_end, xdt)
    chunk_decay = jnp.exp(cum[:, :, -1])

    def step(h, inp):
        st, dec = inp
        return dec[..., None, None] * h + st, h

    h_final, h_prev = lax.scan(step, h0, (jnp.moveaxis(states, 1, 0), jnp.moveaxis(chunk_decay, 1, 0)))
    return jnp.moveaxis(h_prev, 0, 1), h_final


def ssd_chunked(x, dt, a, bm, cm, h0):
    b, l = x.shape[:2]
    xdt, bc, cum = _ssd_prep(x, dt, a, bm)
    cc = cm.astype(jnp.float32).reshape(b, l // CHUNK, CHUNK, N_GROUPS, D_STATE)
    h_prev, h_final = _ssd_inter_chunk(xdt, bc, cum, h0)
    cum_t = jnp.moveaxis(cum, 2, -1)
    seg = cum_t[..., :, None] - cum_t[..., None, :]
    lower = jnp.tril(jnp.ones((CHUNK, CHUNK), dtype=bool))
    decay = jnp.exp(jnp.where(lower, seg, -jnp.inf))
    scores = jnp.einsum('bclgn,bcsgn->bcgls', cc, bc)
    y_diag = jnp.einsum('bcgrls,bcsgrp->bclgrp', scores[:, :, :, None] * decay, xdt)
    y_off = jnp.einsum('bclgn,bcgrpn,bclgr->bclgrp', cc, h_prev, jnp.exp(cum))
    return (y_diag + y_off).reshape(x.shape), h_final


def ssd_final_state(x, dt, a, bm, h0):
    xdt, bc, cum = _ssd_prep(x, dt, a, bm)
    return _ssd_inter_chunk(xdt, bc, cum, h0)[1]


def ssd_inputs(proj, conv_w, conv_b, dt_bias):
    b, l, _ = proj.shape
    xbc = jax.nn.silu(depthwise_conv(proj[..., :XBC_DIM], conv_w, conv_b))
    xs = xbc[..., :D_SSD].reshape(b, l, N_HEADS, HEAD_DIM)
    bm = xbc[..., D_SSD:D_SSD + N_GROUPS * D_STATE].reshape(b, l, N_GROUPS, D_STATE)
    cm = xbc[..., D_SSD + N_GROUPS * D_STATE:].reshape(b, l, N_GROUPS, D_STATE)
    dt_raw = proj[..., XBC_DIM:SSD_IN_COLS].astype(jnp.float32).reshape(b, l, 2, N_HEADS)
    dt = jax.nn.softplus(dt_raw + dt_bias.astype(jnp.float32))
    return xs, bm, cm, dt


def mixer_output(proj, xs, y, d_skip, ssd_norm_g, cm_dw_w, cm_dw_b, cm_ln_g, cm_ln_b, w_out):
    b, l, _ = proj.shape
    y = y + d_skip.astype(jnp.float32)[:, None] * xs.astype(jnp.float32)
    z = proj[..., SSD_IN_COLS:Z_END].astype(jnp.float32)
    y_ssd = rms_norm(y.reshape(b, l, D_SSD) * jax.nn.silu(z), ssd_norm_g).astype(proj.dtype)
    v = proj[..., Z_END:Z_END + D_CONF] * jax.nn.sigmoid(proj[..., Z_END + D_CONF:])
    v = jax.nn.silu(layer_norm(depthwise_conv(v, cm_dw_w, cm_dw_b), cm_ln_g, cm_ln_b))
    return jnp.concatenate([y_ssd, v.astype(proj.dtype)], axis=-1) @ w_out


def token_mixer(u_lat, u_ctx, w_in, conv_w, conv_b, dt_bias, a_log, d_skip, ssd_norm_g,
                cm_dw_w, cm_dw_b, cm_ln_g, cm_ln_b, w_out, with_ctx_out):
    b = u_lat.shape[0]
    a = -jnp.exp(a_log.astype(jnp.float32))
    h0 = jnp.zeros((b, N_GROUPS, HEADS_PER_GROUP, HEAD_DIM, D_STATE), jnp.float32)
    p_ctx = u_ctx @ (w_in if with_ctx_out else w_in[:, :SSD_IN_COLS])
    xs_c, bm_c, cm_c, dt_c = ssd_inputs(p_ctx, conv_w, conv_b, dt_bias)
    if with_ctx_out:
        yf_c, hf_c = ssd_chunked(xs_c, dt_c[:, :, 0], a[0], bm_c, cm_c, h0)
        yb_c, hb_c = ssd_chunked(_rev(xs_c), _rev(dt_c[:, :, 1]), a[1], _rev(bm_c), _rev(cm_c), h0)
        m_ctx = mixer_output(p_ctx, xs_c, yf_c + _rev(yb_c), d_skip, ssd_norm_g,
                             cm_dw_w, cm_dw_b, cm_ln_g, cm_ln_b, w_out)
    else:
        hf_c = ssd_final_state(xs_c, dt_c[:, :, 0], a[0], bm_c, h0)
        hb_c = ssd_final_state(_rev(xs_c), _rev(dt_c[:, :, 1]), a[1], _rev(bm_c), h0)
        m_ctx = None
    p_lat = u_lat @ w_in
    xs_l, bm_l, cm_l, dt_l = ssd_inputs(p_lat, conv_w, conv_b, dt_bias)
    yf_l, _ = ssd_chunked(xs_l, dt_l[:, :, 0], a[0], bm_l, cm_l, hf_c)
    yb_l, _ = ssd_chunked(_rev(xs_l), _rev(dt_l[:, :, 1]), a[1], _rev(bm_l), _rev(cm_l), hb_c)
    m_lat = mixer_output(p_lat, xs_l, yf_l + _rev(yb_l), d_skip, ssd_norm_g,
                         cm_dw_w, cm_dw_b, cm_ln_g, cm_ln_b, w_out)
    return m_lat, m_ctx


def expert_choice_ffn(u, router_w, w_gate, w_up, w_down):
    b, n, d = u.shape
    cap = max(1, CAPACITY_FACTOR * n // N_EXPERTS)
    affinity = jax.nn.softmax((u @ router_w).astype(jnp.float32), axis=-1)
    gate, idx = lax.top_k(jnp.swapaxes(affinity, 1, 2), cap)
    sel = jax.vmap(lambda ub, ib: ub[ib])(u, idx)
    hdn = jax.nn.silu(jnp.einsum('becd,edf->becf', sel, w_gate)) * jnp.einsum('becd,edf->becf', sel, w_up)
    out = jnp.einsum('becf,efd->becd', hdn, w_down) * gate[..., None]
    y = jax.vmap(lambda ib, ob: jnp.zeros((n, d), ob.dtype).at[ib.reshape(-1)].add(ob.reshape(-1, d)))(idx, out)
    return y.astype(u.dtype)


def setup_inputs(seed: int = 0) -> dict:
    key = jax.random.key(seed)
    ks = jax.random.split(key, 26)
    f32 = jnp.float32
    nrm = lambda k, shape, s: jax.random.normal(k, shape, f32) * s
    dt_init = jnp.exp(jax.random.uniform(ks[10], (DEPTH, 2, N_HEADS), f32, np.log(1e-3), np.log(1e-1)))
    return {
        'x': nrm(ks[0], (BATCH, SEQ, D_MODEL), 1.0),
        'c': nrm(ks[1], (BATCH, D_MODEL), 1.0),
        'ctx': nrm(ks[2], (BATCH, CTX_LEN, D_MODEL), 1.0),
        'c_ctx': nrm(ks[3], (D_MODEL,), 1.0),
        'ada_w': nrm(ks[4], (DEPTH, D_MODEL, 6 * D_MODEL), 0.5 * D_MODEL ** -0.5),
        'ada_b': nrm(ks[5], (DEPTH, 6 * D_MODEL), 0.01),
        'mix_pre_g': 1.0 + nrm(ks[6], (DEPTH, D_MODEL), 0.05),
        'w_in': nrm(ks[7], (DEPTH, D_MODEL, IN_COLS), D_MODEL ** -0.5),
        'conv_w': nrm(ks[8], (DEPTH, SSD_CONV, XBC_DIM), SSD_CONV ** -0.5),
        'conv_b': nrm(ks[9], (DEPTH, XBC_DIM), 0.01),
        'dt_bias': dt_init + jnp.log(-jnp.expm1(-dt_init)),
        'a_log': jnp.log(jax.random.uniform(ks[11], (DEPTH, 2, N_HEADS), f32, 1.0, 16.0)),
        'd_skip': 1.0 + nrm(ks[12], (DEPTH, N_HEADS), 0.1),
        'ssd_norm_g': 1.0 + nrm(ks[13], (DEPTH, D_SSD), 0.05),
        'cm_dw_w': nrm(ks[14], (DEPTH, CONF_KERNEL, D_CONF), CONF_KERNEL ** -0.5),
        'cm_dw_b': nrm(ks[15], (DEPTH, D_CONF), 0.01),
        'cm_ln_g': 1.0 + nrm(ks[16], (DEPTH, D_CONF), 0.05),
        'cm_ln_b': nrm(ks[17], (DEPTH, D_CONF), 0.01),
        'w_out': nrm(ks[18], (DEPTH, D_MIX, D_MODEL), D_MIX ** -0.5),
        'mix_post_g': 1.0 + nrm(ks[19], (DEPTH, D_MODEL), 0.05),
        'moe_pre_g': 1.0 + nrm(ks[20], (DEPTH, D_MODEL), 0.05),
        'router_w': nrm(ks[21], (DEPTH, D_MODEL, N_EXPERTS), D_MODEL ** -0.5),
        'exp_w_gate': nrm(ks[22], (DEPTH, N_EXPERTS, D_MODEL, D_EXPERT), D_MODEL ** -0.5),
        'exp_w_up': nrm(ks[23], (DEPTH, N_EXPERTS, D_MODEL, D_EXPERT), D_MODEL ** -0.5),
        'exp_w_down': nrm(ks[24], (DEPTH, N_EXPERTS, D_EXPERT, D_MODEL), D_EXPERT ** -0.5),
        'moe_post_g': 1.0 + nrm(ks[25], (DEPTH, D_MODEL), 0.05),
    }


def reference(x, c, ctx, c_ctx, ada_w, ada_b, mix_pre_g, w_in, conv_w, conv_b, dt_bias, a_log,
              d_skip, ssd_norm_g, cm_dw_w, cm_dw_b, cm_ln_g, cm_ln_b, w_out, mix_post_g,
              moe_pre_g, router_w, exp_w_gate, exp_w_up, exp_w_down, moe_post_g):
    for i in range(DEPTH):
        last = i == DEPTH - 1
        sh1, sc1, g1, sh2, sc2, g2 = [m[:, None, :] for m in adaln(c, ada_w[i], ada_b[i])]
        csh1, csc1, cg1, csh2, csc2, cg2 = adaln(c_ctx, ada_w[i], ada_b[i])
        u_lat = rms_norm(x, mix_pre_g[i]) * (1.0 + sc1) + sh1
        u_ctx = rms_norm(ctx, mix_pre_g[i]) * (1.0 + csc1) + csh1
        m_lat, m_ctx = token_mixer(u_lat, u_ctx, w_in[i], conv_w[i], conv_b[i], dt_bias[i], a_log[i],
                                   d_skip[i], ssd_norm_g[i], cm_dw_w[i], cm_dw_b[i], cm_ln_g[i],
                                   cm_ln_b[i], w_out[i], not last)
        x = x + g1 * rms_norm(m_lat, mix_post_g[i])
        u = rms_norm(x, moe_pre_g[i]) * (1.0 + sc2) + sh2
        x = x + g2 * rms_norm(expert_choice_ffn(u, router_w[i], exp_w_gate[i], exp_w_up[i], exp_w_down[i]), moe_post_g[i])
        if not last:
            ctx = ctx + cg1 * rms_norm(m_ctx, mix_post_g[i])
            uc = rms_norm(ctx, moe_pre_g[i]) * (1.0 + csc2) + csh2
            ctx = ctx + cg2 * rms_norm(expert_choice_ffn(uc, router_w[i], exp_w_gate[i], exp_w_up[i], exp_w_down[i]), moe_post_g[i])
    return x
```

```python
import functools

import jax
import jax.numpy as jnp
import numpy as np
from jax import lax
from jax.experimental import pallas as pl
from jax.experimental.pallas import tpu as pltpu

D_MODEL = 1024
DEPTH = 2
CTX_LEN = 256
D_SSD = 1024
HEAD_DIM = 64
N_HEADS = D_SSD // HEAD_DIM
N_GROUPS = 4
HEADS_PER_GROUP = N_HEADS // N_GROUPS
D_STATE = 128
CHUNK = 128
D_CONF = 1024
D_MIX = D_SSD + D_CONF
XBC_DIM = D_SSD + 2 * N_GROUPS * D_STATE
SSD_IN_COLS = XBC_DIM + 2 * N_HEADS
Z_END = SSD_IN_COLS + D_SSD
IN_COLS = Z_END + 2 * D_CONF
N_EXPERTS = 16
CAPACITY_FACTOR = 2
D_EXPERT = 1408
EPS = 1e-6

VMEM_LIMIT_BYTES = 56 * 1024 * 1024


def _mm_kernel(a_ref, w_ref, o_ref):
    a = a_ref[...].astype(jnp.bfloat16)
    o_ref[...] = jnp.dot(a, w_ref[...], preferred_element_type=jnp.float32)


def _matmul(a, w, tm=512, tn=512):
    m, k = a.shape
    n = w.shape[1]
    tm = min(tm, m)
    n_pad = (-n) % 128
    if n_pad:
        w = jnp.pad(w, ((0, 0), (0, n_pad)))
    np_ = n + n_pad
    tn = min(tn, np_)
    while np_ % tn:
        tn -= 128
    out = pl.pallas_call(
        _mm_kernel,
        grid=(m // tm, np_ // tn),
        in_specs=[pl.BlockSpec((tm, k), lambda i, j: (i, 0)),
                  pl.BlockSpec((k, tn), lambda i, j: (0, j))],
        out_specs=pl.BlockSpec((tm, tn), lambda i, j: (i, j)),
        out_shape=jax.ShapeDtypeStruct((m, np_), jnp.float32),
        compiler_params=pltpu.CompilerParams(
            dimension_semantics=("parallel", "parallel"),
            vmem_limit_bytes=VMEM_LIMIT_BYTES),
        name="dense_matmul",
    )(a, w)
    return out[:, :n] if n_pad else out


def _ffn_kernel(sel_ref, gate_ref, wg_ref, wu_ref, wd_ref, o_ref):
    s = sel_ref[0, 0].astype(jnp.bfloat16)
    hg = jnp.dot(s, wg_ref[0], preferred_element_type=jnp.float32)
    hu = jnp.dot(s, wu_ref[0], preferred_element_type=jnp.float32)
    h = (hg * jax.nn.sigmoid(hg) * hu).astype(jnp.bfloat16)
    out = jnp.dot(h, wd_ref[0], preferred_element_type=jnp.float32)
    o_ref[0, 0] = out * gate_ref[0, 0]


def _expert_ffn(sel, gate, wg, wu, wd):
    b, e, cap, d = sel.shape
    f = wg.shape[-1]
    return pl.pallas_call(
        _ffn_kernel,
        grid=(e, b),
        in_specs=[pl.BlockSpec((1, 1, cap, d), lambda ei, bi: (bi, ei, 0, 0)),
                  pl.BlockSpec((1, 1, cap, 1), lambda ei, bi: (bi, ei, 0, 0)),
                  pl.BlockSpec((1, d, f), lambda ei, bi: (ei, 0, 0)),
                  pl.BlockSpec((1, d, f), lambda ei, bi: (ei, 0, 0)),
                  pl.BlockSpec((1, f, d), lambda ei, bi: (ei, 0, 0))],
        out_specs=pl.BlockSpec((1, 1, cap, d), lambda ei, bi: (bi, ei, 0, 0)),
        out_shape=jax.ShapeDtypeStruct((b, e, cap, d), jnp.float32),
        compiler_params=pltpu.CompilerParams(
            dimension_semantics=("parallel", "arbitrary"),
            vmem_limit_bytes=VMEM_LIMIT_BYTES),
        name="expert_ffn",
    )(sel, gate, wg, wu, wd)


def _rms_norm(x, g):
    return x * lax.rsqrt(jnp.mean(x * x, axis=-1, keepdims=True) + EPS) * g


def _layer_norm(x, g, b):
    xc = x - jnp.mean(x, axis=-1, keepdims=True)
    var = jnp.mean(xc * xc, axis=-1, keepdims=True)
    return xc * lax.rsqrt(var + EPS) * g + b


def _depthwise_conv(u, w, bias):
    k = w.shape[0]
    out = lax.conv_general_dilated(u, w[:, None, :], window_strides=(1,),
                                   padding=[(k // 2, k // 2)],
                                   dimension_numbers=('NWC', 'WIO', 'NWC'),
                                   feature_group_count=u.shape[-1],
                                   precision=lax.Precision.HIGHEST)
    return out + bias


def _adaln(cond, ada_w, ada_b):
    return jnp.split(jnp.dot(jax.nn.silu(cond), ada_w, precision=lax.Precision.HIGHEST) + ada_b, 6, axis=-1)


def _rev(t):
    return jnp.flip(t, axis=1)


_HI = lax.Precision.HIGHEST


def _ssd_prep(x, dt, a, bm):
    b, l = x.shape[:2]
    nc = l // CHUNK
    xdt = (x * dt[..., None]).reshape(b, nc, CHUNK, N_GROUPS, HEADS_PER_GROUP, HEAD_DIM)
    bc = bm.reshape(b, nc, CHUNK, N_GROUPS, D_STATE)
    cum = jnp.cumsum((dt * a).reshape(b, nc, CHUNK, N_GROUPS, HEADS_PER_GROUP), axis=2)
    return xdt, bc, cum


def _ssd_inter_chunk(xdt, bc, cum, h0):
    to_end = jnp.exp(cum[:, :, -1:] - cum)
    states = jnp.einsum('bclgn,bclgr,bclgrp->bcgrpn', bc, to_end, xdt)
    chunk_decay = jnp.exp(cum[:, :, -1])

    def step(h, inp):
        st, dec = inp
        return dec[..., None, None] * h + st, h

    h_final, h_prev = lax.scan(step, h0, (jnp.moveaxis(states, 1, 0), jnp.moveaxis(chunk_decay, 1, 0)))
    return jnp.moveaxis(h_prev, 0, 1), h_final


def _ssd_chunked(x, dt, a, bm, cm, h0):
    b, l = x.shape[:2]
    xdt, bc, cum = _ssd_prep(x, dt, a, bm)
    cc = cm.reshape(b, l // CHUNK, CHUNK, N_GROUPS, D_STATE)
    h_prev, h_final = _ssd_inter_chunk(xdt, bc, cum, h0)
    cum_t = jnp.moveaxis(cum, 2, -1)
    seg = cum_t[..., :, None] - cum_t[..., None, :]
    lower = jnp.tril(jnp.ones((CHUNK, CHUNK), dtype=bool))
    decay = jnp.exp(jnp.where(lower, seg, -jnp.inf))
    scores = jnp.einsum('bclgn,bcsgn->bcgls', cc, bc)
    y_diag = jnp.einsum('bcgrls,bcsgrp->bclgrp', scores[:, :, :, None] * decay, xdt)
    y_off = jnp.einsum('bclgn,bcgrpn,bclgr->bclgrp', cc, h_prev, jnp.exp(cum))
    return (y_diag + y_off).reshape(x.shape), h_final


def _ssd_final_state(x, dt, a, bm, h0):
    xdt, bc, cum = _ssd_prep(x, dt, a, bm)
    return _ssd_inter_chunk(xdt, bc, cum, h0)[1]


def _ssd_inputs(proj, conv_w, conv_b, dt_bias):
    b, l, _ = proj.shape
    xbc = jax.nn.silu(_depthwise_conv(proj[..., :XBC_DIM], conv_w, conv_b))
    xs = xbc[..., :D_SSD].reshape(b, l, N_HEADS, HEAD_DIM)
    bm = xbc[..., D_SSD:D_SSD + N_GROUPS * D_STATE].reshape(b, l, N_GROUPS, D_STATE)
    cm = xbc[..., D_SSD + N_GROUPS * D_STATE:].reshape(b, l, N_GROUPS, D_STATE)
    dt_raw = proj[..., XBC_DIM:SSD_IN_COLS].reshape(b, l, 2, N_HEADS)
    dt = jax.nn.softplus(dt_raw + dt_bias)
    return xs, bm, cm, dt


def _proj(u, w_bf16):
    b, l, d = u.shape
    return _matmul(u.reshape(b * l, d), w_bf16).reshape(b, l, -1)


def _mixer_output(proj, xs, y, d_skip, ssd_norm_g, cm_dw_w, cm_dw_b, cm_ln_g, cm_ln_b, w_out_bf16):
    b, l, _ = proj.shape
    y = y + d_skip[:, None] * xs
    z = proj[..., SSD_IN_COLS:Z_END]
    y_ssd = _rms_norm(y.reshape(b, l, D_SSD) * jax.nn.silu(z), ssd_norm_g)
    v = proj[..., Z_END:Z_END + D_CONF] * jax.nn.sigmoid(proj[..., Z_END + D_CONF:])
    v = jax.nn.silu(_layer_norm(_depthwise_conv(v, cm_dw_w, cm_dw_b), cm_ln_g, cm_ln_b))
    return _proj(jnp.concatenate([y_ssd, v], axis=-1), w_out_bf16)


def _token_mixer(u_lat, u_ctx, w_in, conv_w, conv_b, dt_bias, a_log, d_skip, ssd_norm_g,
                 cm_dw_w, cm_dw_b, cm_ln_g, cm_ln_b, w_out, with_ctx_out):
    b = u_lat.shape[0]
    a = -jnp.exp(a_log)
    h0 = jnp.zeros((b, N_GROUPS, HEADS_PER_GROUP, HEAD_DIM, D_STATE), jnp.float32)
    p_ctx = _proj(u_ctx, w_in if with_ctx_out else w_in[:, :SSD_IN_COLS])
    xs_c, bm_c, cm_c, dt_c = _ssd_inputs(p_ctx, conv_w, conv_b, dt_bias)
    if with_ctx_out:
        yf_c, hf_c = _ssd_chunked(xs_c, dt_c[:, :, 0], a[0], bm_c, cm_c, h0)
        yb_c, hb_c = _ssd_chunked(_rev(xs_c), _rev(dt_c[:, :, 1]), a[1], _rev(bm_c), _rev(cm_c), h0)
        m_ctx = _mixer_output(p_ctx, xs_c, yf_c + _rev(yb_c), d_skip, ssd_norm_g,
                              cm_dw_w, cm_dw_b, cm_ln_g, cm_ln_b, w_out)
    else:
        hf_c = _ssd_final_state(xs_c, dt_c[:, :, 0], a[0], bm_c, h0)
        hb_c = _ssd_final_state(_rev(xs_c), _rev(dt_c[:, :, 1]), a[1], _rev(bm_c), h0)
        m_ctx = None
    p_lat = _proj(u_lat, w_in)
    xs_l, bm_l, cm_l, dt_l = _ssd_inputs(p_lat, conv_w, conv_b, dt_bias)
    yf_l, _ = _ssd_chunked(xs_l, dt_l[:, :, 0], a[0], bm_l, cm_l, hf_c)
    yb_l, _ = _ssd_chunked(_rev(xs_l), _rev(dt_l[:, :, 1]), a[1], _rev(bm_l), _rev(cm_l), hb_c)
    m_lat = _mixer_output(p_lat, xs_l, yf_l + _rev(yb_l), d_skip, ssd_norm_g,
                          cm_dw_w, cm_dw_b, cm_ln_g, cm_ln_b, w_out)
    return m_lat, m_ctx


def _expert_choice_ffn(u, router_w, wg, wu, wd):
    b, n, d = u.shape
    cap = max(1, CAPACITY_FACTOR * n // N_EXPERTS)
    logits = _proj(u, router_w)
    affinity = jax.nn.softmax(logits, axis=-1)
    gate, idx = lax.top_k(jnp.swapaxes(affinity, 1, 2), cap)
    sel = jax.vmap(lambda ub, ib: ub[ib])(u, idx)
    out = _expert_ffn(sel, gate[..., None], wg, wu, wd)
    return jax.vmap(lambda ib, ob: jnp.zeros((n, d), ob.dtype).at[ib.reshape(-1)].add(ob.reshape(-1, d)))(idx, out)


def kernel(x, c, ctx, c_ctx, ada_w, ada_b, mix_pre_g, w_in, conv_w, conv_b, dt_bias, a_log, d_skip,
           ssd_norm_g, cm_dw_w, cm_dw_b, cm_ln_g, cm_ln_b, w_out, mix_post_g, moe_pre_g, router_w,
           exp_w_gate, exp_w_up, exp_w_down, moe_post_g):
    bf = jnp.bfloat16
    for i in range(DEPTH):
        last = i == DEPTH - 1
        sh1, sc1, g1, sh2, sc2, g2 = [m[:, None, :] for m in _adaln(c, ada_w[i], ada_b[i])]
        csh1, csc1, cg1, csh2, csc2, cg2 = _adaln(c_ctx, ada_w[i], ada_b[i])
        w_in_i, w_out_i, router_i = w_in[i].astype(bf), w_out[i].astype(bf), router_w[i].astype(bf)
        wg, wu, wd = exp_w_gate[i].astype(bf), exp_w_up[i].astype(bf), exp_w_down[i].astype(bf)
        u_lat = _rms_norm(x, mix_pre_g[i]) * (1.0 + sc1) + sh1
        u_ctx = _rms_norm(ctx, mix_pre_g[i]) * (1.0 + csc1) + csh1
        m_lat, m_ctx = _token_mixer(u_lat, u_ctx, w_in_i, conv_w[i], conv_b[i], dt_bias[i], a_log[i],
                                    d_skip[i], ssd_norm_g[i], cm_dw_w[i], cm_dw_b[i], cm_ln_g[i],
                                    cm_ln_b[i], w_out_i, not last)
        x = x + g1 * _rms_norm(m_lat, mix_post_g[i])
        u = _rms_norm(x, moe_pre_g[i]) * (1.0 + sc2) + sh2
        x = x + g2 * _rms_norm(_expert_choice_ffn(u, router_i, wg, wu, wd), moe_post_g[i])
        if not last:
            ctx = ctx + cg1 * _rms_norm(m_ctx, mix_post_g[i])
            uc = _rms_norm(ctx, moe_pre_g[i]) * (1.0 + csc2) + csh2
            ctx = ctx + cg2 * _rms_norm(_expert_choice_ffn(uc, router_i, wg, wu, wd), moe_post_g[i])
    return x
```

```python
import functools

import jax
import jax.numpy as jnp
from jax import lax
from jax.experimental import pallas as pl
from jax.experimental.pallas import tpu as pltpu

D_MODEL = 1024
DEPTH = 2
CTX_LEN = 256
D_SSD = 1024
HEAD_DIM = 64
N_HEADS = D_SSD // HEAD_DIM
N_GROUPS = 4
HEADS_PER_GROUP = N_HEADS // N_GROUPS
D_STATE = 128
D_CONF = 1024
SSD_CONV = 5
CONF_KERNEL = 31
BC_DIM = N_GROUPS * D_STATE
XBC_DIM = D_SSD + 2 * BC_DIM
SSD_IN_COLS = XBC_DIM + 2 * N_HEADS
Z_END = SSD_IN_COLS + D_SSD
N_EXPERTS = 16
CAPACITY_FACTOR = 2
EPS = 1e-6

LANES = 128
SUBLANES = 8
BF16_ROWS = 16
TM = 256
CHUNK = 128
VMEM_LIMIT_BYTES = 56 * 1024 * 1024

F32 = jnp.float32
BF16 = jnp.bfloat16


def _dot(a, b):
    return jnp.dot(a, b, preferred_element_type=F32)


def _sigmoid(x):
    return 1.0 / (1.0 + jnp.exp(-x))


def _params(sem):
    return pltpu.CompilerParams(dimension_semantics=sem, vmem_limit_bytes=VMEM_LIMIT_BYTES)


def _mm_kernel(a_ref, w_ref, o_ref):
    o_ref[...] = _dot(a_ref[...].astype(BF16), w_ref[...].astype(BF16))


def _matmul(a, w, tn=512):
    m, k = a.shape
    n = w.shape[1]
    return pl.pallas_call(
        _mm_kernel,
        grid=(n // tn,),
        in_specs=[pl.BlockSpec((m, k), lambda j: (0, 0)),
                  pl.BlockSpec((k, tn), lambda j: (0, j))],
        out_specs=pl.BlockSpec((m, tn), lambda j: (0, j)),
        out_shape=jax.ShapeDtypeStruct((m, n), F32),
        compiler_params=_params(("parallel",)),
        name="adaln_matmul",
    )(a, w)


def _mix_in_kernel(x_ref, scale_ref, shift_ref, wxbc_ref, wdtf_ref, wdtb_ref, wz_ref, wval_ref,
                   wgate_ref, dtbf_ref, dtbb_ref, xbc_ref, dt_ref, zs_ref, v_ref):
    x = x_ref[0]
    ms = jnp.mean(x * x, axis=-1, keepdims=True)
    u = x * lax.rsqrt(ms + EPS) * scale_ref[0, 0] + shift_ref[0, 0]
    ub = u.astype(BF16)
    xbc_ref[0] = _dot(ub, wxbc_ref[...]).astype(BF16)

    def softplus(t):
        return jnp.maximum(t, 0.0) + jnp.log(1.0 + jnp.exp(-jnp.abs(t)))

    dt_ref[0, 0] = softplus(_dot(ub, wdtf_ref[...]) + dtbf_ref[...])
    dt_ref[1, 0] = softplus(_dot(ub, wdtb_ref[...]) + dtbb_ref[...])
    z = _dot(ub, wz_ref[...])
    zs_ref[0] = (z * _sigmoid(z)).astype(BF16)
    val = _dot(ub, wval_ref[...])
    gate = _dot(ub, wgate_ref[...])
    v_ref[0] = (val * _sigmoid(gate)).astype(BF16)


def _seg(t):
    return jnp.minimum(t, 1)


def _mix_in(xc, scale, shift, wxbc, wdtf, wdtb, wz, wval, wgate, dtbf, dtbb):
    b, s, d = xc.shape
    full = lambda shape: pl.BlockSpec(shape, lambda bi, t: (0,) * len(shape))
    mod = pl.BlockSpec((1, 1, 1, d), lambda bi, t: (bi, _seg(t), 0, 0))
    return pl.pallas_call(
        _mix_in_kernel,
        grid=(b, s // TM),
        in_specs=[pl.BlockSpec((1, TM, d), lambda bi, t: (bi, t, 0)), mod, mod,
                  full(wxbc.shape), full(wdtf.shape), full(wdtb.shape), full(wz.shape),
                  full(wval.shape), full(wgate.shape), full(dtbf.shape), full(dtbb.shape)],
        out_specs=[pl.BlockSpec((1, TM, XBC_DIM), lambda bi, t: (bi, t, 0)),
                   pl.BlockSpec((2, 1, TM, LANES), lambda bi, t: (0, bi, t, 0)),
                   pl.BlockSpec((1, TM, D_SSD), lambda bi, t: (bi, t, 0)),
                   pl.BlockSpec((1, TM, D_CONF), lambda bi, t: (bi, t, 0))],
        out_shape=[jax.ShapeDtypeStruct((b, s, XBC_DIM), BF16),
                   jax.ShapeDtypeStruct((2, b, s, LANES), F32),
                   jax.ShapeDtypeStruct((b, s, D_SSD), BF16),
                   jax.ShapeDtypeStruct((b, s, D_CONF), BF16)],
        compiler_params=_params(("parallel", "parallel")),
        name="mix_in",
    )(xc, scale, shift, wxbc, wdtf, wdtb, wz, wval, wgate, dtbf, dtbb)


def _halo_specs(width, n_tiles):
    per_tile = TM // BF16_ROWS
    n_blocks = n_tiles * per_tile
    prev = pl.BlockSpec((1, BF16_ROWS, width),
                        lambda bi, t: (bi, jnp.maximum(t * per_tile - 1, 0), 0))
    nxt = pl.BlockSpec((1, BF16_ROWS, width),
                       lambda bi, t: (bi, jnp.minimum((t + 1) * per_tile, n_blocks - 1), 0))
    return prev, nxt


def _fill_ext(ext_ref, prev_ref, cur_ref, next_ref, n_tiles):
    t = pl.program_id(1)
    prev_ok = jnp.where(t >= 2, 1.0, 0.0)
    next_ok = jnp.where(jnp.logical_and(t >= 1, t < n_tiles - 1), 1.0, 0.0)
    ext_ref[0:BF16_ROWS, :] = prev_ref[0].astype(F32) * prev_ok
    ext_ref[BF16_ROWS:BF16_ROWS + TM, :] = cur_ref[0].astype(F32)
    ext_ref[BF16_ROWS + TM:, :] = next_ref[0].astype(F32) * next_ok


CONV_ROWS = 32
CONV_COLS = 512


def _conv_silu_kernel(n_tiles, prev_ref, cur_ref, next_ref, w_ref, b_ref, o_ref, ext_ref):
    _fill_ext(ext_ref, prev_ref, cur_ref, next_ref, n_tiles)
    half = SSD_CONV // 2
    pad = 8

    def row_block(i, carry):
        base = pl.multiple_of(i * CONV_ROWS, CONV_ROWS)
        for c0 in range(0, XBC_DIM, CONV_COLS):
            cols = pl.ds(c0, CONV_COLS)
            win = ext_ref[pl.ds(base + BF16_ROWS - pad, CONV_ROWS + 2 * pad), cols]
            acc = jnp.broadcast_to(b_ref[:, cols], (CONV_ROWS, CONV_COLS))
            for k in range(SSD_CONV):
                o = pad - half + k
                acc = acc + w_ref[k:k + 1, cols] * win[o:o + CONV_ROWS, :]
            o_ref[0, pl.ds(base, CONV_ROWS), cols] = (acc * _sigmoid(acc)).astype(BF16)
        return carry

    lax.fori_loop(0, TM // CONV_ROWS, row_block, 0)


def _conv_silu(xbc, w, bias):
    b, s, c = xbc.shape
    n_tiles = s // TM
    prev, nxt = _halo_specs(c, n_tiles)
    return pl.pallas_call(
        functools.partial(_conv_silu_kernel, n_tiles),
        grid=(b, n_tiles),
        in_specs=[prev, pl.BlockSpec((1, TM, c), lambda bi, t: (bi, t, 0)), nxt,
                  pl.BlockSpec(w.shape, lambda bi, t: (0, 0)),
                  pl.BlockSpec(bias.shape, lambda bi, t: (0, 0))],
        out_specs=pl.BlockSpec((1, TM, c), lambda bi, t: (bi, t, 0)),
        out_shape=jax.ShapeDtypeStruct((b, s, c), BF16),
        scratch_shapes=[pltpu.VMEM((TM + 2 * BF16_ROWS, c), F32)],
        compiler_params=_params(("parallel", "parallel")),
        name="conv_silu",
    )(xbc, xbc, xbc, w, bias)


def _split3(v):
    hi = v.astype(BF16)
    r1 = v - hi.astype(F32)
    mid = r1.astype(BF16)
    lo = (r1 - mid.astype(F32)).astype(BF16)
    return hi, mid, lo


def _ssd_kernel(x_ref, dt_ref, tri_ref, a_ref, dskip_ref, y_ref, state_ref):
    @pl.when(pl.program_id(2) == 0)
    def _():
        state_ref[...] = jnp.zeros_like(state_ref)

    dt = dt_ref[0, 0]
    tri = tri_ref[0]
    da = dt * a_ref[0]
    tri_b = tri.astype(BF16)
    hi, mid, lo = _split3(da)
    cum = _dot(tri_b, hi) + _dot(tri_b, mid) + _dot(tri_b, lo)
    tot = jnp.sum(da, axis=0, keepdims=True)
    expcum = jnp.exp(cum)
    w = dt * jnp.exp(tot - cum)
    dec = jnp.exp(tot)
    cum_t = cum.T
    dt_t = dt.T
    w_t = w.T
    valid = tri > 0.0
    lane = lax.broadcasted_iota(jnp.int32, (CHUNK, 2 * HEAD_DIM), 1)
    first = lane < HEAD_DIM

    for g in range(N_GROUPS):
        b_g = x_ref[0, :, D_SSD + g * D_STATE:D_SSD + (g + 1) * D_STATE]
        c_g = x_ref[0, :, D_SSD + BC_DIM + g * D_STATE:D_SSD + BC_DIM + (g + 1) * D_STATE]
        b_gt = b_g.astype(F32).T
        scores = lax.dot_general(c_g, b_g, (((1,), (1,)), ((), ())), preferred_element_type=F32)
        gcols = slice(g * HEADS_PER_GROUP * HEAD_DIM, (g + 1) * HEADS_PER_GROUP * HEAD_DIM)
        y_off = _dot(c_g, state_ref[:, gcols].astype(BF16))
        for pair in range(HEADS_PER_GROUP // 2):
            h0 = g * HEADS_PER_GROUP + 2 * pair
            cols = slice(h0 * HEAD_DIM, (h0 + 2) * HEAD_DIM)
            x_p = x_ref[0, :, cols]
            y_heads, upd_heads = [], []
            for h in (h0, h0 + 1):
                seg = cum[:, h:h + 1] - cum_t[h:h + 1, :]
                decay = jnp.exp(jnp.where(valid, seg, -1e30))
                m = (scores * decay * dt_t[h:h + 1, :]).astype(BF16)
                y_heads.append(_dot(m, x_p))
                bw = (b_gt * w_t[h:h + 1, :]).astype(BF16)
                upd_heads.append(_dot(bw, x_p))
            y_diag = jnp.where(first, y_heads[0], y_heads[1])
            upd = jnp.where(first, upd_heads[0], upd_heads[1])
            off_scale = jnp.where(first, expcum[:, h0:h0 + 1], expcum[:, h0 + 1:h0 + 2])
            pcols = slice(2 * pair * HEAD_DIM, (2 * pair + 2) * HEAD_DIM)
            y = y_diag + off_scale * y_off[:, pcols] + dskip_ref[0, :, cols] * x_p.astype(F32)
            y_ref[0, 0, :, cols] = y.astype(BF16)
            dec_p = jnp.where(first[0:1, :], dec[:, h0:h0 + 1], dec[:, h0 + 1:h0 + 2])
            state_ref[:, cols] = dec_p * state_ref[:, cols] + upd


def _ssd_scan(xbcs, dt, tri, a_dir, dskip_dir):
    b, s, _ = xbcs.shape
    nch = s // CHUNK
    nctx = CTX_LEN // CHUNK

    def chunk(d, k):
        back = jnp.where(k < nctx, nctx - 1 - k, nch + nctx - 1 - k)
        return jnp.where(d == 0, k, back)

    per_dir = lambda shape: pl.BlockSpec((1,) + shape[1:], lambda bi, d, k: (d,) + (0,) * (len(shape) - 1))
    return pl.pallas_call(
        _ssd_kernel,
        grid=(b, 2, nch),
        in_specs=[pl.BlockSpec((1, CHUNK, XBC_DIM), lambda bi, d, k: (bi, chunk(d, k), 0)),
                  pl.BlockSpec((1, 1, CHUNK, LANES), lambda bi, d, k: (d, bi, chunk(d, k), 0)),
                  per_dir(tri.shape), per_dir(a_dir.shape), per_dir(dskip_dir.shape)],
        out_specs=pl.BlockSpec((1, 1, CHUNK, D_SSD), lambda bi, d, k: (d, bi, chunk(d, k), 0)),
        out_shape=jax.ShapeDtypeStruct((2, b, s, D_SSD), BF16),
        scratch_shapes=[pltpu.VMEM((D_STATE, D_SSD), F32)],
        compiler_params=_params(("parallel", "parallel", "arbitrary")),
        name="ssd_scan",
    )(xbcs, dt, tri, a_dir, dskip_dir)


CM_ROWS = 16


def _mix_out_kernel(n_tiles, yf_ref, yb_ref, zs_ref, vprev_ref, v_ref, vnext_ref, x_ref, ng_ref, cw_ref,
                    cb_ref, lg_ref, lb_ref, w1_ref, w2_ref, pg_ref, gate_ref, scale_ref, shift_ref,
                    rw_ref, xo_ref, u_ref, lo_ref, sh_ref, vv_ref):
    _fill_ext(sh_ref.at[0], vprev_ref, v_ref, vnext_ref, n_tiles)
    shifted_rows = TM + 2 * BF16_ROWS - SUBLANES
    for r in range(1, SUBLANES):
        sh_ref[r, 0:shifted_rows, :] = sh_ref[0, r:r + shifted_rows, :]
    half = CONF_KERNEL // 2

    def row_block(i, carry):
        base = pl.multiple_of(i * CM_ROWS, CM_ROWS)
        acc = jnp.broadcast_to(cb_ref[...], (CM_ROWS, D_CONF))
        for k in range(CONF_KERNEL):
            q, r = divmod(BF16_ROWS - half + k, SUBLANES)
            acc = acc + cw_ref[k:k + 1, :] * sh_ref[r, pl.ds(base + q * SUBLANES, CM_ROWS), :]
        mean = jnp.mean(acc, axis=-1, keepdims=True)
        xc = acc - mean
        var = jnp.mean(xc * xc, axis=-1, keepdims=True)
        ln = xc * lax.rsqrt(var + EPS) * lg_ref[...] + lb_ref[...]
        vv_ref[pl.ds(base, CM_ROWS), :] = (ln * _sigmoid(ln)).astype(BF16)
        return carry

    lax.fori_loop(0, TM // CM_ROWS, row_block, 0)

    yz = (yf_ref[0, 0].astype(F32) + yb_ref[0, 0].astype(F32)) * zs_ref[0].astype(F32)
    y_ssd = yz * lax.rsqrt(jnp.mean(yz * yz, axis=-1, keepdims=True) + EPS) * ng_ref[...]
    m = _dot(y_ssd.astype(BF16), w1_ref[...]) + _dot(vv_ref[...], w2_ref[...])
    m = m * lax.rsqrt(jnp.mean(m * m, axis=-1, keepdims=True) + EPS) * pg_ref[...]
    xn = x_ref[0] + gate_ref[0, 0] * m
    xo_ref[0] = xn
    u = xn * lax.rsqrt(jnp.mean(xn * xn, axis=-1, keepdims=True) + EPS) * scale_ref[0, 0] + shift_ref[0, 0]
    ub = u.astype(BF16)
    u_ref[0] = ub
    lo_ref[0] = _dot(ub, rw_ref[...])


def _mix_out(y, zs, v, xc, ng, cw, cb, lg, lb, w1, w2, pg, gate, scale, shift, rw):
    b, s, d = xc.shape
    n_tiles = s // TM
    prev, nxt = _halo_specs(D_CONF, n_tiles)
    full = lambda a: pl.BlockSpec(a.shape, lambda bi, t: (0,) * a.ndim)
    mod = pl.BlockSpec((1, 1, 1, d), lambda bi, t: (bi, _seg(t), 0, 0))
    tile = lambda width: pl.BlockSpec((1, TM, width), lambda bi, t: (bi, t, 0))
    return pl.pallas_call(
        functools.partial(_mix_out_kernel, n_tiles),
        grid=(b, n_tiles),
        in_specs=[pl.BlockSpec((1, 1, TM, D_SSD), lambda bi, t: (0, bi, t, 0)),
                  pl.BlockSpec((1, 1, TM, D_SSD), lambda bi, t: (1, bi, t, 0)),
                  tile(D_SSD), prev, tile(D_CONF), nxt, tile(d),
                  full(ng), full(cw), full(cb), full(lg), full(lb), full(w1), full(w2), full(pg),
                  mod, mod, mod, full(rw)],
        out_specs=[tile(d), tile(d), tile(LANES)],
        out_shape=[jax.ShapeDtypeStruct((b, s, d), F32),
                   jax.ShapeDtypeStruct((b, s, d), BF16),
                   jax.ShapeDtypeStruct((b, s, LANES), F32)],
        scratch_shapes=[pltpu.VMEM((SUBLANES, TM + 2 * BF16_ROWS, D_CONF), F32),
                        pltpu.VMEM((TM, D_CONF), BF16)],
        compiler_params=_params(("parallel", "parallel")),
        name="mix_out",
    )(y, y, zs, v, v, v, xc, ng, cw, cb, lg, lb, w1, w2, pg, gate, scale, shift, rw)


def _ffn_kernel(sel_ref, gate_ref, wg_ref, wu_ref, wd_ref, o_ref):
    s = sel_ref[0, 0]
    hg = _dot(s, wg_ref[0])
    hu = _dot(s, wu_ref[0])
    h = (hg * _sigmoid(hg) * hu).astype(BF16)
    o_ref[0, 0] = _dot(h, wd_ref[0]) * gate_ref[0, 0]


def _expert_ffn(sel, gate, wg, wu, wd):
    b, e, cap, d = sel.shape
    f = wg.shape[-1]
    return pl.pallas_call(
        _ffn_kernel,
        grid=(e, b),
        in_specs=[pl.BlockSpec((1, 1, cap, d), lambda ei, bi: (bi, ei, 0, 0)),
                  pl.BlockSpec((1, 1, cap, 1), lambda ei, bi: (bi, ei, 0, 0)),
                  pl.BlockSpec((1, d, f), lambda ei, bi: (ei, 0, 0)),
                  pl.BlockSpec((1, d, f), lambda ei, bi: (ei, 0, 0)),
                  pl.BlockSpec((1, f, d), lambda ei, bi: (ei, 0, 0))],
        out_specs=pl.BlockSpec((1, 1, cap, d), lambda ei, bi: (bi, ei, 0, 0)),
        out_shape=jax.ShapeDtypeStruct((b, e, cap, d), F32),
        compiler_params=_params(("parallel", "arbitrary")),
        name="expert_ffn",
    )(sel, gate, wg, wu, wd)


def _expert_choice_ffn(u, logits, wg, wu, wd):
    b, n, d = u.shape
    cap = max(1, CAPACITY_FACTOR * n // N_EXPERTS)
    affinity = jax.nn.softmax(logits, axis=-1)
    gate, idx = lax.top_k(jnp.swapaxes(affinity, 1, 2), cap)
    sel = jax.vmap(lambda ub, ib: ub[ib])(u, idx)
    out = _expert_ffn(sel, gate[..., None], wg, wu, wd)
    return jax.vmap(lambda ib, ob: jnp.zeros((n, d), ob.dtype).at[ib.reshape(-1)].add(ob.reshape(-1, d)))(idx, out)


def _rms_norm(x, g):
    return x * lax.rsqrt(jnp.mean(x * x, axis=-1, keepdims=True) + EPS) * g


def _pad_cols(w, n):
    return jnp.pad(w, ((0, 0), (0, n - w.shape[1])))


def kernel(x, c, ctx, c_ctx, ada_w, ada_b, mix_pre_g, w_in, conv_w, conv_b, dt_bias, a_log, d_skip,
           ssd_norm_g, cm_dw_w, cm_dw_b, cm_ln_g, cm_ln_b, w_out, mix_post_g, moe_pre_g, router_w,
           exp_w_gate, exp_w_up, exp_w_down, moe_post_g):
    b = x.shape[0]
    xc = jnp.concatenate([ctx, x], axis=1)
    row = lambda v: v.reshape(1, -1)

    li = lax.broadcasted_iota(jnp.int32, (CHUNK, CHUNK), 0)
    si = lax.broadcasted_iota(jnp.int32, (CHUNK, CHUNK), 1)
    tri = jnp.stack([(si <= li), (si >= li)]).astype(F32)

    cond = jnp.zeros((8, D_MODEL), F32).at[:b].set(c).at[b].set(c_ctx)
    cond = cond * _sigmoid(cond)

    for i in range(DEPTH):
        last = i == DEPTH - 1
        mods = _matmul(cond, ada_w[i]) + ada_b[i]
        mods = mods.reshape(8, 6, D_MODEL)
        both = lambda j: jnp.stack([jnp.broadcast_to(mods[b, j], (b, D_MODEL)), mods[:b, j]], axis=1)[:, :, None, :]
        sh1, sc1, g1, sh2, sc2, g2 = [both(j) for j in range(6)]

        wi = w_in[i].astype(BF16)
        wxbc = wi[:, :XBC_DIM]
        wdtf = _pad_cols(wi[:, XBC_DIM:XBC_DIM + N_HEADS], LANES)
        wdtb = _pad_cols(wi[:, XBC_DIM + N_HEADS:SSD_IN_COLS], LANES)
        wz = wi[:, SSD_IN_COLS:Z_END]
        wval = wi[:, Z_END:Z_END + D_CONF]
        wgate = wi[:, Z_END + D_CONF:]
        dtbf = _pad_cols(row(dt_bias[i, 0]), LANES)
        dtbb = _pad_cols(row(dt_bias[i, 1]), LANES)
        xbc, dt, zs, v = _mix_in(xc, mix_pre_g[i] * (1.0 + sc1), sh1, wxbc, wdtf, wdtb, wz, wval, wgate,
                                 dtbf, dtbb)

        xbcs = _conv_silu(xbc, conv_w[i], row(conv_b[i]))

        a_dir = _pad_cols(-jnp.exp(a_log[i]), LANES)[:, None, :]
        dskip = jnp.repeat(d_skip[i], HEAD_DIM)
        dskip_dir = jnp.stack([dskip, jnp.zeros_like(dskip)])[:, None, :]
        y = _ssd_scan(xbcs, dt, tri, a_dir, dskip_dir)

        wo = w_out[i].astype(BF16)
        rw = _pad_cols(router_w[i].astype(BF16), LANES)
        xc, u, logits = _mix_out(y, zs, v, xc, row(ssd_norm_g[i]), cm_dw_w[i], row(cm_dw_b[i]),
                                 row(cm_ln_g[i]), row(cm_ln_b[i]), wo[:D_SSD], wo[D_SSD:],
                                 row(mix_post_g[i]), g1, moe_pre_g[i] * (1.0 + sc2), sh2, rw)

        wg, wu, wd = exp_w_gate[i].astype(BF16), exp_w_up[i].astype(BF16), exp_w_down[i].astype(BF16)
        logits = logits[..., :N_EXPERTS]
        y_lat = _expert_choice_ffn(u[:, CTX_LEN:], logits[:, CTX_LEN:], wg, wu, wd)
        if not last:
            y_ctx = _expert_choice_ffn(u[:, :CTX_LEN], logits[:, :CTX_LEN], wg, wu, wd)
        else:
            y_ctx = jnp.zeros((b, CTX_LEN, D_MODEL), F32)
        y_moe = jnp.concatenate([y_ctx, y_lat], axis=1)
        g2_rows = jnp.concatenate([jnp.broadcast_to(g2[:, 0], (b, CTX_LEN, D_MODEL)),
                                   jnp.broadcast_to(g2[:, 1], (b, xc.shape[1] - CTX_LEN, D_MODEL))], axis=1)
        xc = xc + g2_rows * _rms_norm(y_moe, moe_post_g[i])
    return xc[:, CTX_LEN:]
```

```python
import functools

import jax
import jax.numpy as jnp
from jax import lax
from jax.experimental import pallas as pl
from jax.experimental.pallas import tpu as pltpu

D_MODEL = 1024
DEPTH = 2
CTX_LEN = 256
D_SSD = 1024
HEAD_DIM = 64
N_HEADS = D_SSD // HEAD_DIM
N_GROUPS = 4
HEADS_PER_GROUP = N_HEADS // N_GROUPS
D_STATE = 128
D_CONF = 1024
SSD_CONV = 5
CONF_KERNEL = 31
BC_DIM = N_GROUPS * D_STATE
XBC_DIM = D_SSD + 2 * BC_DIM
SSD_IN_COLS = XBC_DIM + 2 * N_HEADS
Z_END = SSD_IN_COLS + D_SSD
N_EXPERTS = 16
CAPACITY_FACTOR = 2
EPS = 1e-6

LANES = 128
SUBLANES = 8
BF16_ROWS = 16
TM = 256
CHUNK = 128
VMEM_LIMIT_BYTES = 56 * 1024 * 1024

F32 = jnp.float32
BF16 = jnp.bfloat16


def _dot(a, b):
    return jnp.dot(a, b, preferred_element_type=F32)


def _sigmoid(x):
    return 1.0 / (1.0 + jnp.exp(-x))


def _params(sem):
    return pltpu.CompilerParams(dimension_semantics=sem, vmem_limit_bytes=VMEM_LIMIT_BYTES)


def _mm_kernel(a_ref, w_ref, o_ref):
    o_ref[...] = _dot(a_ref[...].astype(BF16), w_ref[...].astype(BF16))


def _matmul(a, w, tn=512):
    m, k = a.shape
    n = w.shape[1]
    return pl.pallas_call(
        _mm_kernel,
        grid=(n // tn,),
        in_specs=[pl.BlockSpec((m, k), lambda j: (0, 0)),
                  pl.BlockSpec((k, tn), lambda j: (0, j))],
        out_specs=pl.BlockSpec((m, tn), lambda j: (0, j)),
        out_shape=jax.ShapeDtypeStruct((m, n), F32),
        compiler_params=_params(("parallel",)),
        name="adaln_matmul",
    )(a, w)


def _mix_in_kernel(x_ref, scale_ref, shift_ref, wxbc_ref, wdtf_ref, wdtb_ref, wz_ref, wval_ref,
                   wgate_ref, dtbf_ref, dtbb_ref, xbc_ref, dt_ref, zs_ref, v_ref):
    x = x_ref[0]
    ms = jnp.mean(x * x, axis=-1, keepdims=True)
    u = x * lax.rsqrt(ms + EPS) * scale_ref[0, 0] + shift_ref[0, 0]
    ub = u.astype(BF16)
    xbc_ref[0] = _dot(ub, wxbc_ref[...]).astype(BF16)

    def softplus(t):
        return jnp.maximum(t, 0.0) + jnp.log(1.0 + jnp.exp(-jnp.abs(t)))

    dt_ref[0, 0] = softplus(_dot(ub, wdtf_ref[...]) + dtbf_ref[...])
    dt_ref[1, 0] = softplus(_dot(ub, wdtb_ref[...]) + dtbb_ref[...])
    z = _dot(ub, wz_ref[...])
    zs_ref[0] = (z * _sigmoid(z)).astype(BF16)
    val = _dot(ub, wval_ref[...])
    gate = _dot(ub, wgate_ref[...])
    v_ref[0] = (val * _sigmoid(gate)).astype(BF16)


def _seg(t):
    return jnp.minimum(t, 1)


def _mix_in(xc, scale, shift, wxbc, wdtf, wdtb, wz, wval, wgate, dtbf, dtbb):
    b, s, d = xc.shape
    full = lambda shape: pl.BlockSpec(shape, lambda bi, t: (0,) * len(shape))
    mod = pl.BlockSpec((1, 1, 1, d), lambda bi, t: (bi, _seg(t), 0, 0))
    return pl.pallas_call(
        _mix_in_kernel,
        grid=(b, s // TM),
        in_specs=[pl.BlockSpec((1, TM, d), lambda bi, t: (bi, t, 0)), mod, mod,
                  full(wxbc.shape), full(wdtf.shape), full(wdtb.shape), full(wz.shape),
                  full(wval.shape), full(wgate.shape), full(dtbf.shape), full(dtbb.shape)],
        out_specs=[pl.BlockSpec((1, TM, XBC_DIM), lambda bi, t: (bi, t, 0)),
                   pl.BlockSpec((2, 1, TM, LANES), lambda bi, t: (0, bi, t, 0)),
                   pl.BlockSpec((1, TM, D_SSD), lambda bi, t: (bi, t, 0)),
                   pl.BlockSpec((1, TM, D_CONF), lambda bi, t: (bi, t, 0))],
        out_shape=[jax.ShapeDtypeStruct((b, s, XBC_DIM), BF16),
                   jax.ShapeDtypeStruct((2, b, s, LANES), F32),
                   jax.ShapeDtypeStruct((b, s, D_SSD), BF16),
                   jax.ShapeDtypeStruct((b, s, D_CONF), BF16)],
        compiler_params=_params(("parallel", "parallel")),
        name="mix_in",
    )(xc, scale, shift, wxbc, wdtf, wdtb, wz, wval, wgate, dtbf, dtbb)


def _halo_specs(width, n_tiles):
    per_tile = TM // BF16_ROWS
    n_blocks = n_tiles * per_tile
    prev = pl.BlockSpec((1, BF16_ROWS, width),
                        lambda bi, t: (bi, jnp.maximum(t * per_tile - 1, 0), 0))
    nxt = pl.BlockSpec((1, BF16_ROWS, width),
                       lambda bi, t: (bi, jnp.minimum((t + 1) * per_tile, n_blocks - 1), 0))
    return prev, nxt


def _fill_ext(ext_ref, prev_ref, cur_ref, next_ref, n_tiles):
    t = pl.program_id(1)
    prev_ok = jnp.where(t >= 2, 1.0, 0.0)
    next_ok = jnp.where(jnp.logical_and(t >= 1, t < n_tiles - 1), 1.0, 0.0)
    ext_ref[0:BF16_ROWS, :] = prev_ref[0].astype(F32) * prev_ok
    ext_ref[BF16_ROWS:BF16_ROWS + TM, :] = cur_ref[0].astype(F32)
    ext_ref[BF16_ROWS + TM:, :] = next_ref[0].astype(F32) * next_ok


CONV_ROWS = 32
CONV_COLS = 512


def _conv_silu_kernel(n_tiles, prev_ref, cur_ref, next_ref, w_ref, b_ref, o_ref, ext_ref):
    _fill_ext(ext_ref, prev_ref, cur_ref, next_ref, n_tiles)
    half = SSD_CONV // 2
    pad = 8

    def row_block(i, carry):
        base = pl.multiple_of(i * CONV_ROWS, CONV_ROWS)
        for c0 in range(0, XBC_DIM, CONV_COLS):
            cols = pl.ds(c0, CONV_COLS)
            win = ext_ref[pl.ds(base + BF16_ROWS - pad, CONV_ROWS + 2 * pad), cols]
            acc = jnp.broadcast_to(b_ref[:, cols], (CONV_ROWS, CONV_COLS))
            for k in range(SSD_CONV):
                o = pad - half + k
                acc = acc + w_ref[k:k + 1, cols] * win[o:o + CONV_ROWS, :]
            o_ref[0, pl.ds(base, CONV_ROWS), cols] = (acc * _sigmoid(acc)).astype(BF16)
        return carry

    lax.fori_loop(0, TM // CONV_ROWS, row_block, 0)


def _conv_silu(xbc, w, bias):
    b, s, c = xbc.shape
    n_tiles = s // TM
    prev, nxt = _halo_specs(c, n_tiles)
    return pl.pallas_call(
        functools.partial(_conv_silu_kernel, n_tiles),
        grid=(b, n_tiles),
        in_specs=[prev, pl.BlockSpec((1, TM, c), lambda bi, t: (bi, t, 0)), nxt,
                  pl.BlockSpec(w.shape, lambda bi, t: (0, 0)),
                  pl.BlockSpec(bias.shape, lambda bi, t: (0, 0))],
        out_specs=pl.BlockSpec((1, TM, c), lambda bi, t: (bi, t, 0)),
        out_shape=jax.ShapeDtypeStruct((b, s, c), BF16),
        scratch_shapes=[pltpu.VMEM((TM + 2 * BF16_ROWS, c), F32)],
        compiler_params=_params(("parallel", "parallel")),
        name="conv_silu",
    )(xbc, xbc, xbc, w, bias)


def _split3(v):
    hi = v.astype(BF16)
    r1 = v - hi.astype(F32)
    mid = r1.astype(BF16)
    lo = (r1 - mid.astype(F32)).astype(BF16)
    return hi, mid, lo


def _ssd_kernel(x_ref, dt_ref, tri_ref, a_ref, dskip_ref, y_ref, state_ref):
    @pl.when(pl.program_id(2) == 0)
    def _():
        state_ref[...] = jnp.zeros_like(state_ref)

    dt = dt_ref[0, 0]
    tri = tri_ref[0]
    da = dt * a_ref[0]
    tri_b = tri.astype(BF16)
    hi, mid, lo = _split3(da)
    cum = _dot(tri_b, hi) + _dot(tri_b, mid) + _dot(tri_b, lo)
    tot = jnp.sum(da, axis=0, keepdims=True)
    expcum = jnp.exp(cum)
    w = dt * jnp.exp(tot - cum)
    dec = jnp.exp(tot)
    cum_t = cum.T
    dt_t = dt.T
    w_t = w.T
    valid = tri > 0.0
    lane = lax.broadcasted_iota(jnp.int32, (CHUNK, 2 * HEAD_DIM), 1)
    first = lane < HEAD_DIM

    for g in range(N_GROUPS):
        b_g = x_ref[0, :, D_SSD + g * D_STATE:D_SSD + (g + 1) * D_STATE]
        c_g = x_ref[0, :, D_SSD + BC_DIM + g * D_STATE:D_SSD + BC_DIM + (g + 1) * D_STATE]
        b_gt = b_g.astype(F32).T
        scores = lax.dot_general(c_g, b_g, (((1,), (1,)), ((), ())), preferred_element_type=F32)
        gcols = slice(g * HEADS_PER_GROUP * HEAD_DIM, (g + 1) * HEADS_PER_GROUP * HEAD_DIM)
        y_off = _dot(c_g, state_ref[:, gcols].astype(BF16))
        for pair in range(HEADS_PER_GROUP // 2):
            h0 = g * HEADS_PER_GROUP + 2 * pair
            cols = slice(h0 * HEAD_DIM, (h0 + 2) * HEAD_DIM)
            x_p = x_ref[0, :, cols]
            y_heads, upd_heads = [], []
            for h in (h0, h0 + 1):
                seg = cum[:, h:h + 1] - cum_t[h:h + 1, :]
                decay = jnp.exp(jnp.where(valid, seg, -1e30))
                m = (scores * decay * dt_t[h:h + 1, :]).astype(BF16)
                y_heads.append(_dot(m, x_p))
                bw = (b_gt * w_t[h:h + 1, :]).astype(BF16)
                upd_heads.append(_dot(bw, x_p))
            y_diag = jnp.where(first, y_heads[0], y_heads[1])
            upd = jnp.where(first, upd_heads[0], upd_heads[1])
            off_scale = jnp.where(first, expcum[:, h0:h0 + 1], expcum[:, h0 + 1:h0 + 2])
            pcols = slice(2 * pair * HEAD_DIM, (2 * pair + 2) * HEAD_DIM)
            y = y_diag + off_scale * y_off[:, pcols] + dskip_ref[0, :, cols] * x_p.astype(F32)
            y_ref[0, 0, :, cols] = y.astype(BF16)
            dec_p = jnp.where(first[0:1, :], dec[:, h0:h0 + 1], dec[:, h0 + 1:h0 + 2])
            state_ref[:, cols] = dec_p * state_ref[:, cols] + upd


def _ssd_scan(xbcs, dt, tri, a_dir, dskip_dir):
    b, s, _ = xbcs.shape
    nch = s // CHUNK
    nctx = CTX_LEN // CHUNK

    def chunk(d, k):
        back = jnp.where(k < nctx, nctx - 1 - k, nch + nctx - 1 - k)
        return jnp.where(d == 0, k, back)

    per_dir = lambda shape: pl.BlockSpec((1,) + shape[1:], lambda bi, d, k: (d,) + (0,) * (len(shape) - 1))
    return pl.pallas_call(
        _ssd_kernel,
        grid=(b, 2, nch),
        in_specs=[pl.BlockSpec((1, CHUNK, XBC_DIM), lambda bi, d, k: (bi, chunk(d, k), 0)),
                  pl.BlockSpec((1, 1, CHUNK, LANES), lambda bi, d, k: (d, bi, chunk(d, k), 0)),
                  per_dir(tri.shape), per_dir(a_dir.shape), per_dir(dskip_dir.shape)],
        out_specs=pl.BlockSpec((1, 1, CHUNK, D_SSD), lambda bi, d, k: (d, bi, chunk(d, k), 0)),
        out_shape=jax.ShapeDtypeStruct((2, b, s, D_SSD), BF16),
        scratch_shapes=[pltpu.VMEM((D_STATE, D_SSD), F32)],
        compiler_params=_params(("parallel", "parallel", "arbitrary")),
        name="ssd_scan",
    )(xbcs, dt, tri, a_dir, dskip_dir)


CM_ROWS = 16


def _mix_out_kernel(n_tiles, yf_ref, yb_ref, zs_ref, vprev_ref, v_ref, vnext_ref, x_ref, ng_ref, cw_ref,
                    cb_ref, lg_ref, lb_ref, w1_ref, w2_ref, pg_ref, gate_ref, scale_ref, shift_ref,
                    rw_ref, xo_ref, u_ref, lo_ref, sh_ref, vv_ref):
    _fill_ext(sh_ref.at[0], vprev_ref, v_ref, vnext_ref, n_tiles)
    shifted_rows = TM + 2 * BF16_ROWS - SUBLANES
    for r in range(1, SUBLANES):
        sh_ref[r, 0:shifted_rows, :] = sh_ref[0, r:r + shifted_rows, :]
    half = CONF_KERNEL // 2

    def row_block(i, carry):
        base = pl.multiple_of(i * CM_ROWS, CM_ROWS)
        acc = jnp.broadcast_to(cb_ref[...], (CM_ROWS, D_CONF))
        for k in range(CONF_KERNEL):
            q, r = divmod(BF16_ROWS - half + k, SUBLANES)
            acc = acc + cw_ref[k:k + 1, :] * sh_ref[r, pl.ds(base + q * SUBLANES, CM_ROWS), :]
        mean = jnp.mean(acc, axis=-1, keepdims=True)
        xc = acc - mean
        var = jnp.mean(xc * xc, axis=-1, keepdims=True)
        ln = xc * lax.rsqrt(var + EPS) * lg_ref[...] + lb_ref[...]
        vv_ref[pl.ds(base, CM_ROWS), :] = (ln * _sigmoid(ln)).astype(BF16)
        return carry

    lax.fori_loop(0, TM // CM_ROWS, row_block, 0)

    yz = (yf_ref[0, 0].astype(F32) + yb_ref[0, 0].astype(F32)) * zs_ref[0].astype(F32)
    y_ssd = yz * lax.rsqrt(jnp.mean(yz * yz, axis=-1, keepdims=True) + EPS) * ng_ref[...]
    m = _dot(y_ssd.astype(BF16), w1_ref[...]) + _dot(vv_ref[...], w2_ref[...])
    m = m * lax.rsqrt(jnp.mean(m * m, axis=-1, keepdims=True) + EPS) * pg_ref[...]
    xn = x_ref[0] + gate_ref[0, 0] * m
    xo_ref[0] = xn
    u = xn * lax.rsqrt(jnp.mean(xn * xn, axis=-1, keepdims=True) + EPS) * scale_ref[0, 0] + shift_ref[0, 0]
    ub = u.astype(BF16)
    u_ref[0] = ub
    lo_ref[0] = _dot(ub, rw_ref[...])


def _mix_out(y, zs, v, xc, ng, cw, cb, lg, lb, w1, w2, pg, gate, scale, shift, rw):
    b, s, d = xc.shape
    n_tiles = s // TM
    prev, nxt = _halo_specs(D_CONF, n_tiles)
    full = lambda a: pl.BlockSpec(a.shape, lambda bi, t: (0,) * a.ndim)
    mod = pl.BlockSpec((1, 1, 1, d), lambda bi, t: (bi, _seg(t), 0, 0))
    tile = lambda width: pl.BlockSpec((1, TM, width), lambda bi, t: (bi, t, 0))
    return pl.pallas_call(
        functools.partial(_mix_out_kernel, n_tiles),
        grid=(b, n_tiles),
        in_specs=[pl.BlockSpec((1, 1, TM, D_SSD), lambda bi, t: (0, bi, t, 0)),
                  pl.BlockSpec((1, 1, TM, D_SSD), lambda bi, t: (1, bi, t, 0)),
                  tile(D_SSD), prev, tile(D_CONF), nxt, tile(d),
                  full(ng), full(cw), full(cb), full(lg), full(lb), full(w1), full(w2), full(pg),
                  mod, mod, mod, full(rw)],
        out_specs=[tile(d), tile(d), tile(LANES)],
        out_shape=[jax.ShapeDtypeStruct((b, s, d), F32),
                   jax.ShapeDtypeStruct((b, s, d), BF16),
                   jax.ShapeDtypeStruct((b, s, LANES), F32)],
        scratch_shapes=[pltpu.VMEM((SUBLANES, TM + 2 * BF16_ROWS, D_CONF), F32),
                        pltpu.VMEM((TM, D_CONF), BF16)],
        compiler_params=_params(("parallel", "parallel")),
        name="mix_out",
    )(y, y, zs, v, v, v, xc, ng, cw, cb, lg, lb, w1, w2, pg, gate, scale, shift, rw)


FFN_COLS = 256


def _ffn_kernel(sel_ref, wg_ref, wu_ref, wd_ref, o_ref):
    s = sel_ref[0, 0]
    f = wg_ref.shape[-1]
    out = jnp.zeros(o_ref.shape[2:], F32)
    for c0 in range(0, f, FFN_COLS):
        c1 = min(c0 + FFN_COLS, f)
        hg = _dot(s, wg_ref[0, :, c0:c1].astype(BF16))
        hu = _dot(s, wu_ref[0, :, c0:c1].astype(BF16))
        h = (hg * _sigmoid(hg) * hu).astype(BF16)
        out = out + _dot(h, wd_ref[0, c0:c1, :].astype(BF16))
    o_ref[0, 0] = out.astype(BF16)


def _expert_ffn(sel, wg, wu, wd):
    b, e, rows, d = sel.shape
    f = wg.shape[-1]
    return pl.pallas_call(
        _ffn_kernel,
        grid=(e, b),
        in_specs=[pl.BlockSpec((1, 1, rows, d), lambda ei, bi: (bi, ei, 0, 0)),
                  pl.BlockSpec((1, d, f), lambda ei, bi: (ei, 0, 0)),
                  pl.BlockSpec((1, d, f), lambda ei, bi: (ei, 0, 0)),
                  pl.BlockSpec((1, f, d), lambda ei, bi: (ei, 0, 0))],
        out_specs=pl.BlockSpec((1, 1, rows, d), lambda ei, bi: (bi, ei, 0, 0)),
        out_shape=jax.ShapeDtypeStruct((b, e, rows, d), BF16),
        compiler_params=_params(("parallel", "arbitrary")),
        name="expert_ffn",
    )(sel, wg, wu, wd)


RT = 128
BISECT_BITS = 31


def _route_kernel(cap, lg_ref, post_ref, posn_ref, gate_ref, starts_ref, aff_ref, afft_ref):
    n = lg_ref.shape[1]
    nt = n // RT
    lane_n = lax.broadcasted_iota(jnp.int32, (RT, LANES), 1)
    real_n = lane_n < N_EXPERTS

    for i in range(nt):
        rows = slice(i * RT, (i + 1) * RT)
        lg = jnp.where(real_n, lg_ref[0, rows, :], -1e30)
        ex = jnp.exp(lg - jnp.max(lg, axis=-1, keepdims=True))
        aff = ex / jnp.sum(ex, axis=-1, keepdims=True)
        aff_ref[rows, :] = aff
        afft_ref[:, rows] = aff.T[0:N_EXPERTS, :]

    bits = lax.bitcast_convert_type(afft_ref[...], jnp.int32)

    def bisect(i, prefix):
        cand = prefix | jnp.left_shift(jnp.int32(1), BISECT_BITS - 1 - i)
        cnt = jnp.sum(jnp.where(bits >= cand, 1.0, 0.0), axis=1, keepdims=True)
        return jnp.where(cnt >= cap, cand, prefix)

    thr = lax.fori_loop(0, BISECT_BITS, bisect, jnp.zeros((N_EXPERTS, 1), jnp.int32))
    n_gt = jnp.sum(jnp.where(bits > thr, 1.0, 0.0), axis=1, keepdims=True)
    need = cap - n_gt

    r_i = lax.broadcasted_iota(jnp.int32, (RT, RT), 0)
    c_i = lax.broadcasted_iota(jnp.int32, (RT, RT), 1)
    upper = jnp.where(r_i <= c_i, 1.0, 0.0).astype(BF16)
    lower = jnp.where(c_i <= r_i, 1.0, 0.0).astype(BF16)
    lane_s = lax.broadcasted_iota(jnp.int32, (N_EXPERTS, LANES), 1)
    carry_eq = jnp.zeros((N_EXPERTS, 1), F32)
    carry_sel = jnp.zeros((N_EXPERTS, 1), F32)
    starts = jnp.zeros((N_EXPERTS, LANES), F32)
    for i in range(nt):
        cols = slice(i * RT, (i + 1) * RT)
        b_t = bits[:, cols]
        eq = jnp.where(b_t == thr, 1.0, 0.0)
        rank = _dot(eq.astype(BF16), upper) + carry_eq
        sel = jnp.where(jnp.logical_or(b_t > thr, jnp.logical_and(b_t == thr, rank <= need)), 1.0, 0.0)
        pos = _dot(sel.astype(BF16), upper) + carry_sel - sel
        post_ref[0, :, cols] = jnp.where(sel > 0.0, pos, -1.0).astype(jnp.int32)
        starts = jnp.where(lane_s == i, carry_sel, starts)
        carry_eq = carry_eq + jnp.sum(eq, axis=1, keepdims=True)
        carry_sel = carry_sel + jnp.sum(sel, axis=1, keepdims=True)
    starts_ref[0] = starts.astype(jnp.int32)

    sub_s = lax.broadcasted_iota(jnp.int32, (N_EXPERTS, LANES), 0)
    eye = sub_s == lane_s
    thr_row = jnp.sum(jnp.where(eye, lax.bitcast_convert_type(thr, F32), 0.0), axis=0, keepdims=True)
    need_row = jnp.sum(jnp.where(eye, need, 0.0), axis=0, keepdims=True)
    carry_eq = jnp.zeros((1, LANES), F32)
    carry_sel = jnp.zeros((1, LANES), F32)
    for i in range(nt):
        rows = slice(i * RT, (i + 1) * RT)
        aff = aff_ref[rows, :]
        eq_b = jnp.logical_and(aff == thr_row, real_n)
        eq = jnp.where(eq_b, 1.0, 0.0)
        rank = _dot(lower, eq.astype(BF16)) + carry_eq
        sel_b = jnp.logical_and(real_n, jnp.logical_or(aff > thr_row, jnp.logical_and(eq_b, rank <= need_row)))
        sel = jnp.where(sel_b, 1.0, 0.0)
        pos = _dot(lower, sel.astype(BF16)) + carry_sel - sel
        posn_ref[0, rows, :] = jnp.where(sel_b, pos, -1.0).astype(jnp.int32)
        gate_ref[0, rows, :] = jnp.where(sel_b, aff, 0.0)
        carry_eq = carry_eq + jnp.sum(eq, axis=0, keepdims=True)
        carry_sel = carry_sel + jnp.sum(sel, axis=0, keepdims=True)


def _route(logits, cap):
    b, n, _ = logits.shape
    return pl.pallas_call(
        functools.partial(_route_kernel, cap),
        grid=(b,),
        in_specs=[pl.BlockSpec((1, n, LANES), lambda bi: (bi, 0, 0))],
        out_specs=[pl.BlockSpec((1, N_EXPERTS, n), lambda bi: (bi, 0, 0)),
                   pl.BlockSpec((1, n, LANES), lambda bi: (bi, 0, 0)),
                   pl.BlockSpec((1, n, LANES), lambda bi: (bi, 0, 0)),
                   pl.BlockSpec((1, N_EXPERTS, LANES), lambda bi: (bi, 0, 0))],
        out_shape=[jax.ShapeDtypeStruct((b, N_EXPERTS, n), jnp.int32),
                   jax.ShapeDtypeStruct((b, n, LANES), jnp.int32),
                   jax.ShapeDtypeStruct((b, n, LANES), F32),
                   jax.ShapeDtypeStruct((b, N_EXPERTS, LANES), jnp.int32)],
        scratch_shapes=[pltpu.VMEM((n, LANES), F32), pltpu.VMEM((N_EXPERTS, n), F32)],
        compiler_params=_params(("parallel",)),
        name="route",
    )(logits)


GT = 256
MIN_ROWS = 128


def _window(start, align, width, rows):
    if width == rows:
        return 0
    a = jnp.minimum((start // align) * align, rows - width)
    return pl.multiple_of(a, align)


def _gather_kernel(nt_r, starts_ref, post_ref, u_ref, sel_ref, acc_ref):
    bi, ei = pl.program_id(0), pl.program_id(1)
    rows = acc_ref.shape[0]
    n = u_ref.shape[1]
    gt = min(GT, n)
    width = min(gt + BF16_ROWS, rows)
    acc_ref[...] = jnp.zeros_like(acc_ref)
    j = lax.broadcasted_iota(jnp.int32, (width, gt), 0)
    for i in range(n // gt):
        start = starts_ref[(bi * N_EXPERTS + ei) * nt_r + i * (gt // RT)]
        a = _window(start, BF16_ROWS, width, rows)
        p = post_ref[0, 0, :, i * gt:(i + 1) * gt]
        onehot = jnp.where(p - a == j, 1.0, 0.0).astype(BF16)
        acc_ref[pl.ds(a, width), :] += _dot(onehot, u_ref[0, i * gt:(i + 1) * gt, :])
    sel_ref[0, 0] = acc_ref[...].astype(BF16)


def _gather(starts, post, u, rows):
    b, n, d = u.shape
    nt_r = n // RT
    post4 = post.reshape(b, N_EXPERTS, 1, n)
    return pl.pallas_call(
        functools.partial(_gather_kernel, nt_r),
        grid_spec=pltpu.PrefetchScalarGridSpec(
            num_scalar_prefetch=1,
            grid=(b, N_EXPERTS),
            in_specs=[pl.BlockSpec((1, 1, 1, n), lambda bi, ei, st: (bi, ei, 0, 0)),
                      pl.BlockSpec((1, n, d), lambda bi, ei, st: (bi, 0, 0))],
            out_specs=pl.BlockSpec((1, 1, rows, d), lambda bi, ei, st: (bi, ei, 0, 0)),
            scratch_shapes=[pltpu.VMEM((rows, d), F32)]),
        out_shape=jax.ShapeDtypeStruct((b, N_EXPERTS, rows, d), BF16),
        compiler_params=_params(("parallel", "parallel")),
        name="moe_gather",
    )(starts, post4, u)


def _scatter_kernel(nt_r, starts_ref, out_ref, posn_ref, gate_ref, x_ref, pg_ref, g2_ref, xo_ref):
    bi, ti = pl.program_id(0), pl.program_id(1)
    rows = out_ref.shape[2]
    width = min(RT + LANES, rows)
    j = lax.broadcasted_iota(jnp.int32, (RT, width), 1)
    for sub in range(TM // RT):
        r0 = sub * RT
        y = jnp.zeros((RT, D_MODEL), F32)
        for e in range(N_EXPERTS):
            start = starts_ref[(bi * N_EXPERTS + e) * nt_r + ti * (TM // RT) + sub]
            a = _window(start, LANES, width, rows)
            p = posn_ref[0, r0:r0 + RT, e:e + 1]
            onehot = jnp.where(p - a == j, 1.0, 0.0).astype(BF16)
            y = y + gate_ref[0, r0:r0 + RT, e:e + 1] * _dot(onehot, out_ref[0, e, pl.ds(a, width), :])
        y = y * lax.rsqrt(jnp.mean(y * y, axis=-1, keepdims=True) + EPS) * pg_ref[...]
        xo_ref[0, r0:r0 + RT, :] = x_ref[0, r0:r0 + RT, :] + g2_ref[0, 0] * y


def _scatter(starts, out, posn, gate, xc, pg, g2, seg, in_tile0, out_tile0, out_rows, alias):
    b, n, _ = posn.shape
    d = xc.shape[-1]
    rows = out.shape[2]
    nt_r = n // RT
    return pl.pallas_call(
        functools.partial(_scatter_kernel, nt_r),
        grid_spec=pltpu.PrefetchScalarGridSpec(
            num_scalar_prefetch=1,
            grid=(b, n // TM),
            in_specs=[pl.BlockSpec((1, N_EXPERTS, rows, d), lambda bi, ti, st: (bi, 0, 0, 0)),
                      pl.BlockSpec((1, TM, LANES), lambda bi, ti, st: (bi, ti, 0)),
                      pl.BlockSpec((1, TM, LANES), lambda bi, ti, st: (bi, ti, 0)),
                      pl.BlockSpec((1, TM, d), lambda bi, ti, st: (bi, ti + in_tile0, 0)),
                      pl.BlockSpec((1, d), lambda bi, ti, st: (0, 0)),
                      pl.BlockSpec((1, 1, 1, d), lambda bi, ti, st: (bi, seg, 0, 0))],
            out_specs=pl.BlockSpec((1, TM, d), lambda bi, ti, st: (bi, ti + out_tile0, 0))),
        out_shape=jax.ShapeDtypeStruct((b, out_rows, d), F32),
        input_output_aliases={4: 0} if alias else {},
        compiler_params=_params(("parallel", "parallel")),
        name="moe_scatter",
    )(starts, out, posn, gate, xc, pg, g2)


def _moe_segment(xc, u_seg, logits_seg, wg, wu, wd, pg, g2, seg, in_tile0, out_tile0, out_rows, alias):
    b, n, _ = u_seg.shape
    cap = max(1, CAPACITY_FACTOR * n // N_EXPERTS)
    rows = max(cap, MIN_ROWS)
    post, posn, gate, starts = _route(logits_seg, cap)
    starts = starts[:, :, :n // RT].reshape(-1)
    sel = _gather(starts, post, u_seg, rows)
    out = _expert_ffn(sel, wg, wu, wd)
    return _scatter(starts, out, posn, gate, xc, pg, g2, seg, in_tile0, out_tile0, out_rows, alias)


def _pad_cols(w, n):
    return jnp.pad(w, ((0, 0), (0, n - w.shape[1])))


def kernel(x, c, ctx, c_ctx, ada_w, ada_b, mix_pre_g, w_in, conv_w, conv_b, dt_bias, a_log, d_skip,
           ssd_norm_g, cm_dw_w, cm_dw_b, cm_ln_g, cm_ln_b, w_out, mix_post_g, moe_pre_g, router_w,
           exp_w_gate, exp_w_up, exp_w_down, moe_post_g):
    b = x.shape[0]
    xc = jnp.concatenate([ctx, x], axis=1)
    row = lambda v: v.reshape(1, -1)

    li = lax.broadcasted_iota(jnp.int32, (CHUNK, CHUNK), 0)
    si = lax.broadcasted_iota(jnp.int32, (CHUNK, CHUNK), 1)
    tri = jnp.stack([(si <= li), (si >= li)]).astype(F32)

    cond = jnp.zeros((8, D_MODEL), F32).at[:b].set(c).at[b].set(c_ctx)
    cond = cond * _sigmoid(cond)

    for i in range(DEPTH):
        last = i == DEPTH - 1
        mods = _matmul(cond, ada_w[i]) + ada_b[i]
        mods = mods.reshape(8, 6, D_MODEL)
        both = lambda j: jnp.stack([jnp.broadcast_to(mods[b, j], (b, D_MODEL)), mods[:b, j]], axis=1)[:, :, None, :]
        sh1, sc1, g1, sh2, sc2, g2 = [both(j) for j in range(6)]

        wi = w_in[i].astype(BF16)
        wxbc = wi[:, :XBC_DIM]
        wdtf = _pad_cols(wi[:, XBC_DIM:XBC_DIM + N_HEADS], LANES)
        wdtb = _pad_cols(wi[:, XBC_DIM + N_HEADS:SSD_IN_COLS], LANES)
        wz = wi[:, SSD_IN_COLS:Z_END]
        wval = wi[:, Z_END:Z_END + D_CONF]
        wgate = wi[:, Z_END + D_CONF:]
        dtbf = _pad_cols(row(dt_bias[i, 0]), LANES)
        dtbb = _pad_cols(row(dt_bias[i, 1]), LANES)
        xbc, dt, zs, v = _mix_in(xc, mix_pre_g[i] * (1.0 + sc1), sh1, wxbc, wdtf, wdtb, wz, wval, wgate,
                                 dtbf, dtbb)

        xbcs = _conv_silu(xbc, conv_w[i], row(conv_b[i]))

        a_dir = _pad_cols(-jnp.exp(a_log[i]), LANES)[:, None, :]
        dskip = jnp.repeat(d_skip[i], HEAD_DIM)
        dskip_dir = jnp.stack([dskip, jnp.zeros_like(dskip)])[:, None, :]
        y = _ssd_scan(xbcs, dt, tri, a_dir, dskip_dir)

        wo = w_out[i].astype(BF16)
        rw = _pad_cols(router_w[i].astype(BF16), LANES)
        xc, u, logits = _mix_out(y, zs, v, xc, row(ssd_norm_g[i]), cm_dw_w[i], row(cm_dw_b[i]),
                                 row(cm_ln_g[i]), row(cm_ln_b[i]), wo[:D_SSD], wo[D_SSD:],
                                 row(mix_post_g[i]), g1, moe_pre_g[i] * (1.0 + sc2), sh2, rw)

        wg, wu, wd = exp_w_gate[i], exp_w_up[i], exp_w_down[i]
        pg = row(moe_post_g[i])
        ctx_tiles = CTX_LEN // TM
        if not last:
            xc = _moe_segment(xc, u[:, CTX_LEN:], logits[:, CTX_LEN:], wg, wu, wd, pg, g2, 1,
                              ctx_tiles, ctx_tiles, xc.shape[1], True)
            xc = _moe_segment(xc, u[:, :CTX_LEN], logits[:, :CTX_LEN], wg, wu, wd, pg, g2, 0,
                              0, 0, xc.shape[1], True)
        else:
            xc = _moe_segment(xc, u[:, CTX_LEN:], logits[:, CTX_LEN:], wg, wu, wd, pg, g2, 1,
                              ctx_tiles, 0, xc.shape[1] - CTX_LEN, False)
    return xc
```

```python
import functools

import jax
import jax.numpy as jnp
from jax import lax
from jax.experimental import pallas as pl
from jax.experimental.pallas import tpu as pltpu

D_MODEL = 1024
DEPTH = 2
CTX_LEN = 256
D_SSD = 1024
HEAD_DIM = 64
N_HEADS = D_SSD // HEAD_DIM
N_GROUPS = 4
HEADS_PER_GROUP = N_HEADS // N_GROUPS
D_STATE = 128
D_CONF = 1024
SSD_CONV = 5
CONF_KERNEL = 31
BC_DIM = N_GROUPS * D_STATE
XBC_DIM = D_SSD + 2 * BC_DIM
SSD_IN_COLS = XBC_DIM + 2 * N_HEADS
Z_END = SSD_IN_COLS + D_SSD
N_EXPERTS = 16
CAPACITY_FACTOR = 2
EPS = 1e-6

LANES = 128
SUBLANES = 8
BF16_ROWS = 16
TM = 256
CHUNK = 128
VMEM_LIMIT_BYTES = 56 * 1024 * 1024

F32 = jnp.float32
BF16 = jnp.bfloat16


def _dot(a, b):
    return jnp.dot(a, b, preferred_element_type=F32)


def _sigmoid(x):
    return 1.0 / (1.0 + jnp.exp(-x))


def _params(sem):
    return pltpu.CompilerParams(dimension_semantics=sem, vmem_limit_bytes=VMEM_LIMIT_BYTES)


def _mm_kernel(a_ref, w_ref, o_ref):
    o_ref[...] = _dot(a_ref[...].astype(BF16), w_ref[...].astype(BF16))


def _matmul(a, w, tn=512):
    m, k = a.shape
    n = w.shape[1]
    return pl.pallas_call(
        _mm_kernel,
        grid=(n // tn,),
        in_specs=[pl.BlockSpec((m, k), lambda j: (0, 0)),
                  pl.BlockSpec((k, tn), lambda j: (0, j))],
        out_specs=pl.BlockSpec((m, tn), lambda j: (0, j)),
        out_shape=jax.ShapeDtypeStruct((m, n), F32),
        compiler_params=_params(("parallel",)),
        name="adaln_matmul",
    )(a, w)


def _mix_in_kernel(x_ref, scale_ref, shift_ref, wxbc_ref, wdtf_ref, wdtb_ref, wz_ref, wval_ref,
                   wgate_ref, dtbf_ref, dtbb_ref, xbc_ref, dt_ref, zs_ref, v_ref):
    x = x_ref[0]
    ms = jnp.mean(x * x, axis=-1, keepdims=True)
    u = x * lax.rsqrt(ms + EPS) * scale_ref[0, 0] + shift_ref[0, 0]
    ub = u.astype(BF16)
    xbc_ref[0] = _dot(ub, wxbc_ref[...]).astype(BF16)

    def softplus(t):
        return jnp.maximum(t, 0.0) + jnp.log(1.0 + jnp.exp(-jnp.abs(t)))

    dt_ref[0, 0] = softplus(_dot(ub, wdtf_ref[...]) + dtbf_ref[...])
    dt_ref[1, 0] = softplus(_dot(ub, wdtb_ref[...]) + dtbb_ref[...])
    z = _dot(ub, wz_ref[...])
    zs_ref[0] = (z * _sigmoid(z)).astype(BF16)
    val = _dot(ub, wval_ref[...])
    gate = _dot(ub, wgate_ref[...])
    v_ref[0] = (val * _sigmoid(gate)).astype(BF16)


def _seg(t):
    return jnp.minimum(t, 1)


def _mix_in(xc, scale, shift, wxbc, wdtf, wdtb, wz, wval, wgate, dtbf, dtbb):
    b, s, d = xc.shape
    full = lambda shape: pl.BlockSpec(shape, lambda bi, t: (0,) * len(shape))
    mod = pl.BlockSpec((1, 1, 1, d), lambda bi, t: (bi, _seg(t), 0, 0))
    return pl.pallas_call(
        _mix_in_kernel,
        grid=(b, s // TM),
        in_specs=[pl.BlockSpec((1, TM, d), lambda bi, t: (bi, t, 0)), mod, mod,
                  full(wxbc.shape), full(wdtf.shape), full(wdtb.shape), full(wz.shape),
                  full(wval.shape), full(wgate.shape), full(dtbf.shape), full(dtbb.shape)],
        out_specs=[pl.BlockSpec((1, TM, XBC_DIM), lambda bi, t: (bi, t, 0)),
                   pl.BlockSpec((2, 1, TM, LANES), lambda bi, t: (0, bi, t, 0)),
                   pl.BlockSpec((1, TM, D_SSD), lambda bi, t: (bi, t, 0)),
                   pl.BlockSpec((1, TM, D_CONF), lambda bi, t: (bi, t, 0))],
        out_shape=[jax.ShapeDtypeStruct((b, s, XBC_DIM), BF16),
                   jax.ShapeDtypeStruct((2, b, s, LANES), F32),
                   jax.ShapeDtypeStruct((b, s, D_SSD), BF16),
                   jax.ShapeDtypeStruct((b, s, D_CONF), BF16)],
        compiler_params=_params(("parallel", "parallel")),
        name="mix_in",
    )(xc, scale, shift, wxbc, wdtf, wdtb, wz, wval, wgate, dtbf, dtbb)


def _halo_specs(width, n_tiles):
    per_tile = TM // BF16_ROWS
    n_blocks = n_tiles * per_tile
    prev = pl.BlockSpec((1, BF16_ROWS, width),
                        lambda bi, t: (bi, jnp.maximum(t * per_tile - 1, 0), 0))
    nxt = pl.BlockSpec((1, BF16_ROWS, width),
                       lambda bi, t: (bi, jnp.minimum((t + 1) * per_tile, n_blocks - 1), 0))
    return prev, nxt


def _fill_ext(ext_ref, prev_ref, cur_ref, next_ref, n_tiles):
    t = pl.program_id(1)
    prev_ok = jnp.where(t >= 2, 1.0, 0.0)
    next_ok = jnp.where(jnp.logical_and(t >= 1, t < n_tiles - 1), 1.0, 0.0)
    ext_ref[0:BF16_ROWS, :] = prev_ref[0].astype(F32) * prev_ok
    ext_ref[BF16_ROWS:BF16_ROWS + TM, :] = cur_ref[0].astype(F32)
    ext_ref[BF16_ROWS + TM:, :] = next_ref[0].astype(F32) * next_ok


CONV_ROWS = 32
CONV_COLS = 512


def _conv_silu_kernel(n_tiles, prev_ref, cur_ref, next_ref, w_ref, b_ref, o_ref, ext_ref):
    _fill_ext(ext_ref, prev_ref, cur_ref, next_ref, n_tiles)
    half = SSD_CONV // 2
    pad = 8

    def row_block(i, carry):
        base = pl.multiple_of(i * CONV_ROWS, CONV_ROWS)
        for c0 in range(0, XBC_DIM, CONV_COLS):
            cols = pl.ds(c0, CONV_COLS)
            win = ext_ref[pl.ds(base + BF16_ROWS - pad, CONV_ROWS + 2 * pad), cols]
            acc = jnp.broadcast_to(b_ref[:, cols], (CONV_ROWS, CONV_COLS))
            for k in range(SSD_CONV):
                o = pad - half + k
                acc = acc + w_ref[k:k + 1, cols] * win[o:o + CONV_ROWS, :]
            o_ref[0, pl.ds(base, CONV_ROWS), cols] = (acc * _sigmoid(acc)).astype(BF16)
        return carry

    lax.fori_loop(0, TM // CONV_ROWS, row_block, 0)


def _conv_silu(xbc, w, bias):
    b, s, c = xbc.shape
    n_tiles = s // TM
    prev, nxt = _halo_specs(c, n_tiles)
    return pl.pallas_call(
        functools.partial(_conv_silu_kernel, n_tiles),
        grid=(b, n_tiles),
        in_specs=[prev, pl.BlockSpec((1, TM, c), lambda bi, t: (bi, t, 0)), nxt,
                  pl.BlockSpec(w.shape, lambda bi, t: (0, 0)),
                  pl.BlockSpec(bias.shape, lambda bi, t: (0, 0))],
        out_specs=pl.BlockSpec((1, TM, c), lambda bi, t: (bi, t, 0)),
        out_shape=jax.ShapeDtypeStruct((b, s, c), BF16),
        scratch_shapes=[pltpu.VMEM((TM + 2 * BF16_ROWS, c), F32)],
        compiler_params=_params(("parallel", "parallel")),
        name="conv_silu",
    )(xbc, xbc, xbc, w, bias)


def _split3(v):
    hi = v.astype(BF16)
    r1 = v - hi.astype(F32)
    mid = r1.astype(BF16)
    lo = (r1 - mid.astype(F32)).astype(BF16)
    return hi, mid, lo


def _ssd_kernel(xf_ref, xb_ref, dtf_ref, dtb_ref, tri_ref, a_ref, dskip_ref, yf_ref, yb_ref, state_ref):
    @pl.when(pl.program_id(1) == 0)
    def _():
        state_ref[...] = jnp.zeros_like(state_ref)

    _ssd_chunk(xf_ref, dtf_ref, tri_ref.at[0], a_ref.at[0], dskip_ref, yf_ref, state_ref.at[0])
    _ssd_chunk(xb_ref, dtb_ref, tri_ref.at[1], a_ref.at[1], None, yb_ref, state_ref.at[1])


def _ssd_chunk(x_ref, dt_ref, tri_ref, a_ref, dskip_ref, y_ref, state_ref):
    dt = dt_ref[0, 0]
    tri = tri_ref[...]
    da = dt * a_ref[...]
    tri_b = tri.astype(BF16)
    hi, mid, lo = _split3(da)
    cum = _dot(tri_b, hi) + _dot(tri_b, mid) + _dot(tri_b, lo)
    tot = jnp.sum(da, axis=0, keepdims=True)
    expcum = jnp.exp(cum)
    w = dt * jnp.exp(tot - cum)
    dec = jnp.exp(tot)
    cum_t = cum.T
    dt_t = dt.T
    w_t = w.T
    valid = tri > 0.0
    lane = lax.broadcasted_iota(jnp.int32, (CHUNK, 2 * HEAD_DIM), 1)
    first = lane < HEAD_DIM

    for g in range(N_GROUPS):
        b_g = x_ref[0, :, D_SSD + g * D_STATE:D_SSD + (g + 1) * D_STATE]
        c_g = x_ref[0, :, D_SSD + BC_DIM + g * D_STATE:D_SSD + BC_DIM + (g + 1) * D_STATE]
        b_gt = b_g.astype(F32).T
        scores = lax.dot_general(c_g, b_g, (((1,), (1,)), ((), ())), preferred_element_type=F32)
        gcols = slice(g * HEADS_PER_GROUP * HEAD_DIM, (g + 1) * HEADS_PER_GROUP * HEAD_DIM)
        y_off = _dot(c_g, state_ref[:, gcols].astype(BF16))
        for pair in range(HEADS_PER_GROUP // 2):
            h0 = g * HEADS_PER_GROUP + 2 * pair
            cols = slice(h0 * HEAD_DIM, (h0 + 2) * HEAD_DIM)
            x_p = x_ref[0, :, cols]
            y_heads, upd_heads = [], []
            for h in (h0, h0 + 1):
                seg = cum[:, h:h + 1] - cum_t[h:h + 1, :]
                decay = jnp.exp(jnp.where(valid, seg, -1e30))
                m = (scores * decay * dt_t[h:h + 1, :]).astype(BF16)
                y_heads.append(_dot(m, x_p))
                bw = (b_gt * w_t[h:h + 1, :]).astype(BF16)
                upd_heads.append(_dot(bw, x_p))
            y_diag = jnp.where(first, y_heads[0], y_heads[1])
            upd = jnp.where(first, upd_heads[0], upd_heads[1])
            off_scale = jnp.where(first, expcum[:, h0:h0 + 1], expcum[:, h0 + 1:h0 + 2])
            pcols = slice(2 * pair * HEAD_DIM, (2 * pair + 2) * HEAD_DIM)
            y = y_diag + off_scale * y_off[:, pcols]
            if dskip_ref is not None:
                y = y + dskip_ref[:, cols] * x_p.astype(F32)
            y_ref[0, :, cols] = y.astype(BF16)
            dec_p = jnp.where(first[0:1, :], dec[:, h0:h0 + 1], dec[:, h0 + 1:h0 + 2])
            state_ref[:, cols] = dec_p * state_ref[:, cols] + upd


def _ssd_scan(xbcs, dt, tri, a_dir, dskip):
    b, s, _ = xbcs.shape
    nch = s // CHUNK
    nctx = CTX_LEN // CHUNK

    fwd = lambda k: k
    bwd = lambda k: jnp.where(k < nctx, nctx - 1 - k, nch + nctx - 1 - k)
    full = lambda a: pl.BlockSpec(a.shape, lambda bi, k: (0,) * a.ndim)
    x_spec = lambda c: pl.BlockSpec((1, CHUNK, XBC_DIM), lambda bi, k: (bi, c(k), 0))
    dt_spec = lambda d, c: pl.BlockSpec((1, 1, CHUNK, LANES), lambda bi, k: (d, bi, c(k), 0))
    y_spec = lambda c: pl.BlockSpec((1, CHUNK, D_SSD), lambda bi, k: (bi, c(k), 0))
    y_shape = jax.ShapeDtypeStruct((b, s, D_SSD), BF16)
    return pl.pallas_call(
        _ssd_kernel,
        grid=(b, nch),
        in_specs=[x_spec(fwd), x_spec(bwd), dt_spec(0, fwd), dt_spec(1, bwd),
                  full(tri), full(a_dir), full(dskip)],
        out_specs=[y_spec(fwd), y_spec(bwd)],
        out_shape=[y_shape, y_shape],
        scratch_shapes=[pltpu.VMEM((2, D_STATE, D_SSD), F32)],
        compiler_params=_params(("parallel", "arbitrary")),
        name="ssd_scan",
    )(xbcs, xbcs, dt, dt, tri, a_dir, dskip)


CM_ROWS = 32
CM_COLS = 512


def _mix_out_kernel(n_tiles, yf_ref, yb_ref, zs_ref, vprev_ref, v_ref, vnext_ref, x_ref, ng_ref, cw_ref,
                    cb_ref, lg_ref, lb_ref, w1_ref, w2_ref, pg_ref, gate_ref, scale_ref, shift_ref,
                    rw_ref, xo_ref, u_ref, lo_ref, sh_ref, cv_ref):
    _fill_ext(sh_ref.at[0], vprev_ref, v_ref, vnext_ref, n_tiles)
    shifted_rows = TM + 2 * BF16_ROWS - SUBLANES
    for r in range(1, SUBLANES):
        sh_ref[r, 0:shifted_rows, :] = sh_ref[0, r:r + shifted_rows, :]
    half = CONF_KERNEL // 2

    def row_block(i, carry):
        base = pl.multiple_of(i * CM_ROWS, CM_ROWS)
        for c0 in range(0, D_CONF, CM_COLS):
            cols = pl.ds(c0, CM_COLS)
            acc = jnp.broadcast_to(cb_ref[:, cols], (CM_ROWS, CM_COLS))
            for k in range(CONF_KERNEL):
                q, r = divmod(BF16_ROWS - half + k, SUBLANES)
                acc = acc + cw_ref[k:k + 1, cols] * sh_ref[r, pl.ds(base + q * SUBLANES, CM_ROWS), cols]
            cv_ref[pl.ds(base, CM_ROWS), cols] = acc
        return carry

    lax.fori_loop(0, TM // CM_ROWS, row_block, 0)

    cv = cv_ref[...]
    cc = cv - jnp.mean(cv, axis=-1, keepdims=True)
    ln = cc * lax.rsqrt(jnp.mean(cc * cc, axis=-1, keepdims=True) + EPS) * lg_ref[...] + lb_ref[...]
    vv = (ln * _sigmoid(ln)).astype(BF16)

    yz = (yf_ref[0].astype(F32) + yb_ref[0].astype(F32)) * zs_ref[0].astype(F32)
    y_ssd = yz * lax.rsqrt(jnp.mean(yz * yz, axis=-1, keepdims=True) + EPS) * ng_ref[...]
    m = _dot(y_ssd.astype(BF16), w1_ref[...]) + _dot(vv, w2_ref[...])
    m = m * lax.rsqrt(jnp.mean(m * m, axis=-1, keepdims=True) + EPS) * pg_ref[...]
    xn = x_ref[0] + gate_ref[0, 0] * m
    xo_ref[0] = xn
    u = xn * lax.rsqrt(jnp.mean(xn * xn, axis=-1, keepdims=True) + EPS) * scale_ref[0, 0] + shift_ref[0, 0]
    ub = u.astype(BF16)
    u_ref[0] = ub
    lo_ref[0] = _dot(ub, rw_ref[...])


def _mix_out(yf, yb, zs, v, xc, ng, cw, cb, lg, lb, w1, w2, pg, gate, scale, shift, rw):
    b, s, d = xc.shape
    n_tiles = s // TM
    prev, nxt = _halo_specs(D_CONF, n_tiles)
    full = lambda a: pl.BlockSpec(a.shape, lambda bi, t: (0,) * a.ndim)
    mod = pl.BlockSpec((1, 1, 1, d), lambda bi, t: (bi, _seg(t), 0, 0))
    tile = lambda width: pl.BlockSpec((1, TM, width), lambda bi, t: (bi, t, 0))
    return pl.pallas_call(
        functools.partial(_mix_out_kernel, n_tiles),
        grid=(b, n_tiles),
        in_specs=[tile(D_SSD), tile(D_SSD), tile(D_SSD), prev, tile(D_CONF), nxt, tile(d),
                  full(ng), full(cw), full(cb), full(lg), full(lb), full(w1), full(w2), full(pg),
                  mod, mod, mod, full(rw)],
        out_specs=[tile(d), tile(d), tile(LANES)],
        out_shape=[jax.ShapeDtypeStruct((b, s, d), F32),
                   jax.ShapeDtypeStruct((b, s, d), BF16),
                   jax.ShapeDtypeStruct((b, s, LANES), F32)],
        scratch_shapes=[pltpu.VMEM((SUBLANES, TM + 2 * BF16_ROWS, D_CONF), F32),
                        pltpu.VMEM((TM, D_CONF), F32)],
        compiler_params=_params(("parallel", "parallel")),
        name="mix_out",
    )(yf, yb, zs, v, v, v, xc, ng, cw, cb, lg, lb, w1, w2, pg, gate, scale, shift, rw)


FFN_COLS = 256


FFN_MAX_ROWS = 512


def _ffn_kernel(sel_ref, wg_ref, wu_ref, wd_ref, o_ref):
    nb, _, rows, d = sel_ref.shape
    s = sel_ref[:, 0].reshape(nb * rows, d)
    f = wg_ref.shape[-1]
    out = jnp.zeros((nb * rows, d), F32)
    for c0 in range(0, f, FFN_COLS):
        c1 = min(c0 + FFN_COLS, f)
        hg = _dot(s, wg_ref[0, :, c0:c1].astype(BF16))
        hu = _dot(s, wu_ref[0, :, c0:c1].astype(BF16))
        h = (hg * _sigmoid(hg) * hu).astype(BF16)
        out = out + _dot(h, wd_ref[0, c0:c1, :].astype(BF16))
    o_ref[:, 0] = out.astype(BF16).reshape(nb, rows, d)


def _expert_ffn(sel, wg, wu, wd):
    b, e, rows, d = sel.shape
    f = wg.shape[-1]
    nb = b if b * rows <= FFN_MAX_ROWS else 1
    return pl.pallas_call(
        _ffn_kernel,
        grid=(e, b // nb),
        in_specs=[pl.BlockSpec((nb, 1, rows, d), lambda ei, bi: (bi, ei, 0, 0)),
                  pl.BlockSpec((1, d, f), lambda ei, bi: (ei, 0, 0)),
                  pl.BlockSpec((1, d, f), lambda ei, bi: (ei, 0, 0)),
                  pl.BlockSpec((1, f, d), lambda ei, bi: (ei, 0, 0))],
        out_specs=pl.BlockSpec((nb, 1, rows, d), lambda ei, bi: (bi, ei, 0, 0)),
        out_shape=jax.ShapeDtypeStruct((b, e, rows, d), BF16),
        compiler_params=_params(("parallel", "arbitrary")),
        name="expert_ffn",
    )(sel, wg, wu, wd)


RT = 128
BISECT_BITS = 31


def _route_kernel(cap, row0, lg_ref, post_ref, posn_ref, gate_ref, starts_ref, aff_ref, afft_ref):
    n = lg_ref.shape[1] - row0
    nt = n // RT
    lane_n = lax.broadcasted_iota(jnp.int32, (RT, LANES), 1)
    real_n = lane_n < N_EXPERTS

    for i in range(nt):
        rows = slice(i * RT, (i + 1) * RT)
        lg = jnp.where(real_n, lg_ref[0, row0 + i * RT:row0 + (i + 1) * RT, :], -1e30)
        ex = jnp.exp(lg - jnp.max(lg, axis=-1, keepdims=True))
        aff = ex / jnp.sum(ex, axis=-1, keepdims=True)
        aff_ref[rows, :] = aff
        afft_ref[:, rows] = aff.T[0:N_EXPERTS, :]

    bits = lax.bitcast_convert_type(afft_ref[...], jnp.int32)

    def bisect(i, prefix):
        cand = prefix | jnp.left_shift(jnp.int32(1), BISECT_BITS - 1 - i)
        cnt = jnp.sum(jnp.where(bits >= cand, 1.0, 0.0), axis=1, keepdims=True)
        return jnp.where(cnt >= cap, cand, prefix)

    thr = lax.fori_loop(0, BISECT_BITS, bisect, jnp.zeros((N_EXPERTS, 1), jnp.int32))
    n_gt = jnp.sum(jnp.where(bits > thr, 1.0, 0.0), axis=1, keepdims=True)
    need = cap - n_gt

    r_i = lax.broadcasted_iota(jnp.int32, (RT, RT), 0)
    c_i = lax.broadcasted_iota(jnp.int32, (RT, RT), 1)
    upper = jnp.where(r_i <= c_i, 1.0, 0.0).astype(BF16)
    lower = jnp.where(c_i <= r_i, 1.0, 0.0).astype(BF16)
    lane_s = lax.broadcasted_iota(jnp.int32, (N_EXPERTS, LANES), 1)
    carry_eq = jnp.zeros((N_EXPERTS, 1), F32)
    carry_sel = jnp.zeros((N_EXPERTS, 1), F32)
    starts = jnp.zeros((N_EXPERTS, LANES), F32)
    for i in range(nt):
        cols = slice(i * RT, (i + 1) * RT)
        b_t = bits[:, cols]
        eq = jnp.where(b_t == thr, 1.0, 0.0)
        rank = _dot(eq.astype(BF16), upper) + carry_eq
        sel = jnp.where(jnp.logical_or(b_t > thr, jnp.logical_and(b_t == thr, rank <= need)), 1.0, 0.0)
        pos = _dot(sel.astype(BF16), upper) + carry_sel - sel
        post_ref[0, :, cols] = jnp.where(sel > 0.0, pos, -1.0).astype(jnp.int32)
        starts = jnp.where(lane_s == i, carry_sel, starts)
        carry_eq = carry_eq + jnp.sum(eq, axis=1, keepdims=True)
        carry_sel = carry_sel + jnp.sum(sel, axis=1, keepdims=True)
    starts_ref[0] = starts.astype(jnp.int32)

    sub_s = lax.broadcasted_iota(jnp.int32, (N_EXPERTS, LANES), 0)
    eye = sub_s == lane_s
    thr_row = jnp.sum(jnp.where(eye, lax.bitcast_convert_type(thr, F32), 0.0), axis=0, keepdims=True)
    need_row = jnp.sum(jnp.where(eye, need, 0.0), axis=0, keepdims=True)
    carry_eq = jnp.zeros((1, LANES), F32)
    carry_sel = jnp.zeros((1, LANES), F32)
    for i in range(nt):
        rows = slice(i * RT, (i + 1) * RT)
        aff = aff_ref[rows, :]
        eq_b = jnp.logical_and(aff == thr_row, real_n)
        eq = jnp.where(eq_b, 1.0, 0.0)
        rank = _dot(lower, eq.astype(BF16)) + carry_eq
        sel_b = jnp.logical_and(real_n, jnp.logical_or(aff > thr_row, jnp.logical_and(eq_b, rank <= need_row)))
        sel = jnp.where(sel_b, 1.0, 0.0)
        pos = _dot(lower, sel.astype(BF16)) + carry_sel - sel
        posn_ref[0, rows, :] = jnp.where(sel_b, pos, -1.0).astype(jnp.int32)
        gate_ref[0, rows, :] = jnp.where(sel_b, aff, 0.0)
        carry_eq = carry_eq + jnp.sum(eq, axis=0, keepdims=True)
        carry_sel = carry_sel + jnp.sum(sel, axis=0, keepdims=True)


def _route(logits, row0, n, cap):
    b = logits.shape[0]
    return pl.pallas_call(
        functools.partial(_route_kernel, cap, row0),
        grid=(b,),
        in_specs=[pl.BlockSpec((1, row0 + n, LANES), lambda bi: (bi, 0, 0))],
        out_specs=[pl.BlockSpec((1, N_EXPERTS, n), lambda bi: (bi, 0, 0)),
                   pl.BlockSpec((1, n, LANES), lambda bi: (bi, 0, 0)),
                   pl.BlockSpec((1, n, LANES), lambda bi: (bi, 0, 0)),
                   pl.BlockSpec((1, N_EXPERTS, LANES), lambda bi: (bi, 0, 0))],
        out_shape=[jax.ShapeDtypeStruct((b, N_EXPERTS, n), jnp.int32),
                   jax.ShapeDtypeStruct((b, n, LANES), jnp.int32),
                   jax.ShapeDtypeStruct((b, n, LANES), F32),
                   jax.ShapeDtypeStruct((b, N_EXPERTS, LANES), jnp.int32)],
        scratch_shapes=[pltpu.VMEM((n, LANES), F32), pltpu.VMEM((N_EXPERTS, n), F32)],
        compiler_params=_params(("parallel",)),
        name="route",
    )(logits)


GT = 256
MIN_ROWS = 128


def _window(start, align, width, rows):
    if width == rows:
        return 0
    a = jnp.minimum((start // align) * align, rows - width)
    return pl.multiple_of(a, align)


def _gather_kernel(nt_r, row0, starts_ref, post_ref, u_ref, sel_ref, acc_ref):
    bi, ei = pl.program_id(0), pl.program_id(1)
    rows = acc_ref.shape[0]
    n = u_ref.shape[1] - row0
    gt = min(GT, n)
    width = min(gt + BF16_ROWS, rows)
    acc_ref[...] = jnp.zeros_like(acc_ref)
    j = lax.broadcasted_iota(jnp.int32, (width, gt), 0)
    for i in range(n // gt):
        start = starts_ref[(bi * N_EXPERTS + ei) * nt_r + i * (gt // RT)]
        a = _window(start, BF16_ROWS, width, rows)
        p = post_ref[0, 0, :, i * gt:(i + 1) * gt]
        onehot = jnp.where(p - a == j, 1.0, 0.0).astype(BF16)
        acc_ref[pl.ds(a, width), :] += _dot(onehot, u_ref[0, row0 + i * gt:row0 + (i + 1) * gt, :])
    sel_ref[0, 0] = acc_ref[...].astype(BF16)


def _gather(starts, post, u, row0, rows):
    b, _, d = u.shape
    n = post.shape[-1]
    nt_r = n // RT
    post4 = post.reshape(b, N_EXPERTS, 1, n)
    return pl.pallas_call(
        functools.partial(_gather_kernel, nt_r, row0),
        grid_spec=pltpu.PrefetchScalarGridSpec(
            num_scalar_prefetch=1,
            grid=(b, N_EXPERTS),
            in_specs=[pl.BlockSpec((1, 1, 1, n), lambda bi, ei, st: (bi, ei, 0, 0)),
                      pl.BlockSpec((1, row0 + n, d), lambda bi, ei, st: (bi, 0, 0))],
            out_specs=pl.BlockSpec((1, 1, rows, d), lambda bi, ei, st: (bi, ei, 0, 0)),
            scratch_shapes=[pltpu.VMEM((rows, d), F32)]),
        out_shape=jax.ShapeDtypeStruct((b, N_EXPERTS, rows, d), BF16),
        compiler_params=_params(("parallel", "parallel")),
        name="moe_gather",
    )(starts, post4, u)


def _scatter_kernel(nt_r, starts_ref, out_ref, posn_ref, gate_ref, x_ref, pg_ref, g2_ref, xo_ref):
    bi, ti = pl.program_id(0), pl.program_id(1)
    rows = out_ref.shape[2]
    width = min(RT + LANES, rows)
    j = lax.broadcasted_iota(jnp.int32, (RT, width), 1)
    for sub in range(TM // RT):
        r0 = sub * RT
        y = jnp.zeros((RT, D_MODEL), F32)
        for e in range(N_EXPERTS):
            start = starts_ref[(bi * N_EXPERTS + e) * nt_r + ti * (TM // RT) + sub]
            a = _window(start, LANES, width, rows)
            p = posn_ref[0, r0:r0 + RT, e:e + 1]
            onehot = jnp.where(p - a == j, 1.0, 0.0).astype(BF16)
            y = y + gate_ref[0, r0:r0 + RT, e:e + 1] * _dot(onehot, out_ref[0, e, pl.ds(a, width), :])
        y = y * lax.rsqrt(jnp.mean(y * y, axis=-1, keepdims=True) + EPS) * pg_ref[...]
        xo_ref[0, r0:r0 + RT, :] = x_ref[0, r0:r0 + RT, :] + g2_ref[0, 0] * y


def _scatter(starts, out, posn, gate, xc, pg, g2, seg, in_tile0, out_tile0, out_rows, alias):
    b, n, _ = posn.shape
    d = xc.shape[-1]
    rows = out.shape[2]
    nt_r = n // RT
    return pl.pallas_call(
        functools.partial(_scatter_kernel, nt_r),
        grid_spec=pltpu.PrefetchScalarGridSpec(
            num_scalar_prefetch=1,
            grid=(b, n // TM),
            in_specs=[pl.BlockSpec((1, N_EXPERTS, rows, d), lambda bi, ti, st: (bi, 0, 0, 0)),
                      pl.BlockSpec((1, TM, LANES), lambda bi, ti, st: (bi, ti, 0)),
                      pl.BlockSpec((1, TM, LANES), lambda bi, ti, st: (bi, ti, 0)),
                      pl.BlockSpec((1, TM, d), lambda bi, ti, st: (bi, ti + in_tile0, 0)),
                      pl.BlockSpec((1, d), lambda bi, ti, st: (0, 0)),
                      pl.BlockSpec((1, 1, 1, d), lambda bi, ti, st: (bi, seg, 0, 0))],
            out_specs=pl.BlockSpec((1, TM, d), lambda bi, ti, st: (bi, ti + out_tile0, 0))),
        out_shape=jax.ShapeDtypeStruct((b, out_rows, d), F32),
        input_output_aliases={4: 0} if alias else {},
        compiler_params=_params(("parallel", "parallel")),
        name="moe_scatter",
    )(starts, out, posn, gate, xc, pg, g2)


def _moe_segment(xc, u, logits, row0, n, wg, wu, wd, pg, g2, seg, out_tile0, out_rows, alias):
    cap = max(1, CAPACITY_FACTOR * n // N_EXPERTS)
    rows = max(cap, MIN_ROWS)
    post, posn, gate, starts = _route(logits, row0, n, cap)
    starts = starts[:, :, :n // RT].reshape(-1)
    sel = _gather(starts, post, u, row0, rows)
    out = _expert_ffn(sel, wg, wu, wd)
    return _scatter(starts, out, posn, gate, xc, pg, g2, seg, row0 // TM, out_tile0, out_rows, alias)


def _pad_cols(w, n):
    return jnp.pad(w, ((0, 0), (0, n - w.shape[1])))


def kernel(x, c, ctx, c_ctx, ada_w, ada_b, mix_pre_g, w_in, conv_w, conv_b, dt_bias, a_log, d_skip,
           ssd_norm_g, cm_dw_w, cm_dw_b, cm_ln_g, cm_ln_b, w_out, mix_post_g, moe_pre_g, router_w,
           exp_w_gate, exp_w_up, exp_w_down, moe_post_g):
    b = x.shape[0]
    xc = jnp.concatenate([ctx, x], axis=1)
    row = lambda v: v.reshape(1, -1)

    li = lax.broadcasted_iota(jnp.int32, (CHUNK, CHUNK), 0)
    si = lax.broadcasted_iota(jnp.int32, (CHUNK, CHUNK), 1)
    tri = jnp.stack([(si <= li), (si >= li)]).astype(F32)

    cond = jnp.zeros((8, D_MODEL), F32).at[:b].set(c).at[b].set(c_ctx)
    cond = cond * _sigmoid(cond)

    for i in range(DEPTH):
        last = i == DEPTH - 1
        mods = _matmul(cond, ada_w[i]) + ada_b[i]
        mods = mods.reshape(8, 6, D_MODEL)
        both = lambda j: jnp.stack([jnp.broadcast_to(mods[b, j], (b, D_MODEL)), mods[:b, j]], axis=1)[:, :, None, :]
        sh1, sc1, g1, sh2, sc2, g2 = [both(j) for j in range(6)]

        wi = w_in[i].astype(BF16)
        wxbc = wi[:, :XBC_DIM]
        wdtf = _pad_cols(wi[:, XBC_DIM:XBC_DIM + N_HEADS], LANES)
        wdtb = _pad_cols(wi[:, XBC_DIM + N_HEADS:SSD_IN_COLS], LANES)
        wz = wi[:, SSD_IN_COLS:Z_END]
        wval = wi[:, Z_END:Z_END + D_CONF]
        wgate = wi[:, Z_END + D_CONF:]
        dtbf = _pad_cols(row(dt_bias[i, 0]), LANES)
        dtbb = _pad_cols(row(dt_bias[i, 1]), LANES)
        xbc, dt, zs, v = _mix_in(xc, mix_pre_g[i] * (1.0 + sc1), sh1, wxbc, wdtf, wdtb, wz, wval, wgate,
                                 dtbf, dtbb)

        xbcs = _conv_silu(xbc, conv_w[i], row(conv_b[i]))

        a_dir = _pad_cols(-jnp.exp(a_log[i]), LANES)[:, None, :]
        yf, yb = _ssd_scan(xbcs, dt, tri, a_dir, row(jnp.repeat(d_skip[i], HEAD_DIM)))

        wo = w_out[i].astype(BF16)
        rw = _pad_cols(router_w[i].astype(BF16), LANES)
        xc, u, logits = _mix_out(yf, yb, zs, v, xc, row(ssd_norm_g[i]), cm_dw_w[i], row(cm_dw_b[i]),
                                 row(cm_ln_g[i]), row(cm_ln_b[i]), wo[:D_SSD], wo[D_SSD:],
                                 row(mix_post_g[i]), g1, moe_pre_g[i] * (1.0 + sc2), sh2, rw)

        wg, wu, wd = exp_w_gate[i], exp_w_up[i], exp_w_down[i]
        pg = row(moe_post_g[i])
        s_tot = xc.shape[1]
        n_lat = s_tot - CTX_LEN
        if not last:
            xc = _moe_segment(xc, u, logits, CTX_LEN, n_lat, wg, wu, wd, pg, g2, 1, CTX_LEN // TM, s_tot, True)
            xc = _moe_segment(xc, u, logits, 0, CTX_LEN, wg, wu, wd, pg, g2, 0, 0, s_tot, True)
        else:
            xc = _moe_segment(xc, u, logits, CTX_LEN, n_lat, wg, wu, wd, pg, g2, 1, 0, n_lat, False)
    return xc
```

```python
import functools

import jax
import jax.numpy as jnp
from jax import lax
from jax.experimental import pallas as pl
from jax.experimental.pallas import tpu as pltpu

D_MODEL = 1024
DEPTH = 2
CTX_LEN = 256
D_SSD = 1024
HEAD_DIM = 64
N_HEADS = D_SSD // HEAD_DIM
N_GROUPS = 4
HEADS_PER_GROUP = N_HEADS // N_GROUPS
D_STATE = 128
D_CONF = 1024
SSD_CONV = 5
CONF_KERNEL = 31
BC_DIM = N_GROUPS * D_STATE
XBC_DIM = D_SSD + 2 * BC_DIM
SSD_IN_COLS = XBC_DIM + 2 * N_HEADS
Z_END = SSD_IN_COLS + D_SSD
N_EXPERTS = 16
CAPACITY_FACTOR = 2
EPS = 1e-6

LANES = 128
SUBLANES = 8
BF16_ROWS = 16
TM = 256
CHUNK = 128
VMEM_LIMIT_BYTES = 56 * 1024 * 1024

F32 = jnp.float32
BF16 = jnp.bfloat16


def _dot(a, b):
    return jnp.dot(a, b, preferred_element_type=F32)


def _sigmoid(x):
    return 1.0 / (1.0 + jnp.exp(-x))


def _params(sem):
    return pltpu.CompilerParams(dimension_semantics=sem, vmem_limit_bytes=VMEM_LIMIT_BYTES)


def _mm_kernel(a_ref, w_ref, o_ref):
    o_ref[...] = _dot(a_ref[...].astype(BF16), w_ref[...].astype(BF16))


def _matmul(a, w, tn=512):
    m, k = a.shape
    n = w.shape[1]
    return pl.pallas_call(
        _mm_kernel,
        grid=(n // tn,),
        in_specs=[pl.BlockSpec((m, k), lambda j: (0, 0)),
                  pl.BlockSpec((k, tn), lambda j: (0, j))],
        out_specs=pl.BlockSpec((m, tn), lambda j: (0, j)),
        out_shape=jax.ShapeDtypeStruct((m, n), F32),
        compiler_params=_params(("parallel",)),
        name="adaln_matmul",
    )(a, w)


def _mix_in_kernel(x_ref, scale_ref, shift_ref, wxbc_ref, wdtf_ref, wdtb_ref, wz_ref, wval_ref,
                   wgate_ref, dtbf_ref, dtbb_ref, xbc_ref, dt_ref, zs_ref, v_ref):
    x = x_ref[0]
    ms = jnp.mean(x * x, axis=-1, keepdims=True)
    u = x * lax.rsqrt(ms + EPS) * scale_ref[0, 0] + shift_ref[0, 0]
    ub = u.astype(BF16)
    xbc_ref[0] = _dot(ub, wxbc_ref[...]).astype(BF16)

    def softplus(t):
        return jnp.maximum(t, 0.0) + jnp.log(1.0 + jnp.exp(-jnp.abs(t)))

    dt_ref[0, 0] = softplus(_dot(ub, wdtf_ref[...]) + dtbf_ref[...])
    dt_ref[1, 0] = softplus(_dot(ub, wdtb_ref[...]) + dtbb_ref[...])
    z = _dot(ub, wz_ref[...])
    zs_ref[0] = (z * _sigmoid(z)).astype(BF16)
    val = _dot(ub, wval_ref[...])
    gate = _dot(ub, wgate_ref[...])
    v_ref[0] = (val * _sigmoid(gate)).astype(BF16)


def _seg(t):
    return jnp.minimum(t, 1)


def _mix_in(xc, scale, shift, wxbc, wdtf, wdtb, wz, wval, wgate, dtbf, dtbb):
    b, s, d = xc.shape
    full = lambda shape: pl.BlockSpec(shape, lambda bi, t: (0,) * len(shape))
    mod = pl.BlockSpec((1, 1, 1, d), lambda bi, t: (bi, _seg(t), 0, 0))
    return pl.pallas_call(
        _mix_in_kernel,
        grid=(b, s // TM),
        in_specs=[pl.BlockSpec((1, TM, d), lambda bi, t: (bi, t, 0)), mod, mod,
                  full(wxbc.shape), full(wdtf.shape), full(wdtb.shape), full(wz.shape),
                  full(wval.shape), full(wgate.shape), full(dtbf.shape), full(dtbb.shape)],
        out_specs=[pl.BlockSpec((1, TM, XBC_DIM), lambda bi, t: (bi, t, 0)),
                   pl.BlockSpec((2, 1, TM, LANES), lambda bi, t: (0, bi, t, 0)),
                   pl.BlockSpec((1, TM, D_SSD), lambda bi, t: (bi, t, 0)),
                   pl.BlockSpec((1, TM, D_CONF), lambda bi, t: (bi, t, 0))],
        out_shape=[jax.ShapeDtypeStruct((b, s, XBC_DIM), BF16),
                   jax.ShapeDtypeStruct((2, b, s, LANES), F32),
                   jax.ShapeDtypeStruct((b, s, D_SSD), BF16),
                   jax.ShapeDtypeStruct((b, s, D_CONF), BF16)],
        compiler_params=_params(("parallel", "parallel")),
        name="mix_in",
    )(xc, scale, shift, wxbc, wdtf, wdtb, wz, wval, wgate, dtbf, dtbb)


def _halo_specs(width, n_tiles):
    per_tile = TM // BF16_ROWS
    n_blocks = n_tiles * per_tile
    prev = pl.BlockSpec((1, BF16_ROWS, width),
                        lambda bi, t: (bi, jnp.maximum(t * per_tile - 1, 0), 0))
    nxt = pl.BlockSpec((1, BF16_ROWS, width),
                       lambda bi, t: (bi, jnp.minimum((t + 1) * per_tile, n_blocks - 1), 0))
    return prev, nxt


def _fill_ext(ext_ref, prev_ref, cur_ref, next_ref, n_tiles):
    t = pl.program_id(1)
    prev_ok = jnp.where(t >= 2, 1.0, 0.0)
    next_ok = jnp.where(jnp.logical_and(t >= 1, t < n_tiles - 1), 1.0, 0.0)
    ext_ref[0:BF16_ROWS, :] = prev_ref[0].astype(F32) * prev_ok
    ext_ref[BF16_ROWS:BF16_ROWS + TM, :] = cur_ref[0].astype(F32)
    ext_ref[BF16_ROWS + TM:, :] = next_ref[0].astype(F32) * next_ok


CONV_ROWS = 128
CONV_COLS = 512


def _conv_silu_kernel(n_tiles, prev_ref, cur_ref, next_ref, w_ref, b_ref, o_ref, ext_ref):
    t = pl.program_id(1)
    prev_ok = jnp.where(t >= 2, 1.0, 0.0)
    next_ok = jnp.where(jnp.logical_and(t >= 1, t < n_tiles - 1), 1.0, 0.0)
    ext_ref[0:BF16_ROWS, :] = (prev_ref[0].astype(F32) * prev_ok).astype(BF16)
    ext_ref[BF16_ROWS:BF16_ROWS + TM, :] = cur_ref[0]
    ext_ref[BF16_ROWS + TM:, :] = (next_ref[0].astype(F32) * next_ok).astype(BF16)

    half = SSD_CONV // 2
    win_rows = CONV_ROWS + 2 * BF16_ROWS
    r = lax.broadcasted_iota(jnp.int32, ((SSD_CONV - 1) * CONV_ROWS, win_rows), 0)
    c = lax.broadcasted_iota(jnp.int32, ((SSD_CONV - 1) * CONV_ROWS, win_rows), 1)
    tap = jnp.right_shift(r, CONV_ROWS.bit_length() - 1)
    off = tap - half + jnp.where(tap >= half, 1, 0)
    shift = jnp.where(c == (r & (CONV_ROWS - 1)) + BF16_ROWS + off, 1.0, 0.0).astype(BF16)
    side_taps = [k for k in range(SSD_CONV) if k != half]

    for r0 in range(0, TM, CONV_ROWS):
        for c0 in range(0, XBC_DIM, CONV_COLS):
            cols = pl.ds(c0, CONV_COLS)
            shifted = _dot(shift, ext_ref[r0:r0 + win_rows, cols])
            centre = ext_ref[r0 + BF16_ROWS:r0 + BF16_ROWS + CONV_ROWS, cols].astype(F32)
            acc = b_ref[:, cols] + w_ref[half:half + 1, cols] * centre
            for i, k in enumerate(side_taps):
                acc = acc + w_ref[k:k + 1, cols] * shifted[i * CONV_ROWS:(i + 1) * CONV_ROWS, :]
            o_ref[0, r0:r0 + CONV_ROWS, cols] = (acc * _sigmoid(acc)).astype(BF16)


def _conv_silu(xbc, w, bias):
    b, s, c = xbc.shape
    n_tiles = s // TM
    prev, nxt = _halo_specs(c, n_tiles)
    return pl.pallas_call(
        functools.partial(_conv_silu_kernel, n_tiles),
        grid=(b, n_tiles),
        in_specs=[prev, pl.BlockSpec((1, TM, c), lambda bi, t: (bi, t, 0)), nxt,
                  pl.BlockSpec(w.shape, lambda bi, t: (0, 0)),
                  pl.BlockSpec(bias.shape, lambda bi, t: (0, 0))],
        out_specs=pl.BlockSpec((1, TM, c), lambda bi, t: (bi, t, 0)),
        out_shape=jax.ShapeDtypeStruct((b, s, c), BF16),
        scratch_shapes=[pltpu.VMEM((TM + 2 * BF16_ROWS, c), BF16)],
        compiler_params=_params(("parallel", "parallel")),
        name="conv_silu",
    )(xbc, xbc, xbc, w, bias)


def _split3(v):
    hi = v.astype(BF16)
    r1 = v - hi.astype(F32)
    mid = r1.astype(BF16)
    lo = (r1 - mid.astype(F32)).astype(BF16)
    return hi, mid, lo


def _ssd_kernel(xf_ref, xb_ref, dtf_ref, dtb_ref, tri_ref, a_ref, dskip_ref, yf_ref, yb_ref, state_ref):
    @pl.when(pl.program_id(1) == 0)
    def _():
        state_ref[...] = jnp.zeros_like(state_ref)

    _ssd_chunk(xf_ref, dtf_ref, tri_ref.at[0], a_ref.at[0], dskip_ref, yf_ref, state_ref.at[0])
    _ssd_chunk(xb_ref, dtb_ref, tri_ref.at[1], a_ref.at[1], None, yb_ref, state_ref.at[1])


def _ssd_chunk(x_ref, dt_ref, tri_ref, a_ref, dskip_ref, y_ref, state_ref):
    dt = dt_ref[0, 0]
    tri = tri_ref[...]
    da = dt * a_ref[...]
    tri_b = tri.astype(BF16)
    hi, mid, lo = _split3(da)
    cum = _dot(tri_b, hi) + _dot(tri_b, mid) + _dot(tri_b, lo)
    tot = jnp.sum(da, axis=0, keepdims=True)
    expcum = jnp.exp(cum)
    w = dt * jnp.exp(tot - cum)
    dec = jnp.exp(tot)
    cum_t = cum.T
    dt_t = dt.T
    w_t = w.T
    valid = tri > 0.0
    lane = lax.broadcasted_iota(jnp.int32, (CHUNK, 2 * HEAD_DIM), 1)
    first = lane < HEAD_DIM

    for g in range(N_GROUPS):
        b_g = x_ref[0, :, D_SSD + g * D_STATE:D_SSD + (g + 1) * D_STATE]
        c_g = x_ref[0, :, D_SSD + BC_DIM + g * D_STATE:D_SSD + BC_DIM + (g + 1) * D_STATE]
        b_gt = b_g.astype(F32).T
        scores = lax.dot_general(c_g, b_g, (((1,), (1,)), ((), ())), preferred_element_type=F32)
        gcols = slice(g * HEADS_PER_GROUP * HEAD_DIM, (g + 1) * HEADS_PER_GROUP * HEAD_DIM)
        y_off = _dot(c_g, state_ref[:, gcols].astype(BF16))
        for pair in range(HEADS_PER_GROUP // 2):
            h0 = g * HEADS_PER_GROUP + 2 * pair
            cols = slice(h0 * HEAD_DIM, (h0 + 2) * HEAD_DIM)
            x_p = x_ref[0, :, cols]
            y_heads, upd_heads = [], []
            for h in (h0, h0 + 1):
                seg = cum[:, h:h + 1] - cum_t[h:h + 1, :]
                decay = jnp.exp(jnp.where(valid, seg, -1e30))
                m = (scores * decay * dt_t[h:h + 1, :]).astype(BF16)
                y_heads.append(_dot(m, x_p))
                bw = (b_gt * w_t[h:h + 1, :]).astype(BF16)
                upd_heads.append(_dot(bw, x_p))
            y_diag = jnp.where(first, y_heads[0], y_heads[1])
            upd = jnp.where(first, upd_heads[0], upd_heads[1])
            off_scale = jnp.where(first, expcum[:, h0:h0 + 1], expcum[:, h0 + 1:h0 + 2])
            pcols = slice(2 * pair * HEAD_DIM, (2 * pair + 2) * HEAD_DIM)
            y = y_diag + off_scale * y_off[:, pcols]
            if dskip_ref is not None:
                y = y + dskip_ref[:, cols] * x_p.astype(F32)
            y_ref[0, :, cols] = y.astype(BF16)
            dec_p = jnp.where(first[0:1, :], dec[:, h0:h0 + 1], dec[:, h0 + 1:h0 + 2])
            state_ref[:, cols] = dec_p * state_ref[:, cols] + upd


def _ssd_scan(xbcs, dt, tri, a_dir, dskip):
    b, s, _ = xbcs.shape
    nch = s // CHUNK
    nctx = CTX_LEN // CHUNK

    fwd = lambda k: k
    bwd = lambda k: jnp.where(k < nctx, nctx - 1 - k, nch + nctx - 1 - k)
    full = lambda a: pl.BlockSpec(a.shape, lambda bi, k: (0,) * a.ndim)
    x_spec = lambda c: pl.BlockSpec((1, CHUNK, XBC_DIM), lambda bi, k: (bi, c(k), 0))
    dt_spec = lambda d, c: pl.BlockSpec((1, 1, CHUNK, LANES), lambda bi, k: (d, bi, c(k), 0))
    y_spec = lambda c: pl.BlockSpec((1, CHUNK, D_SSD), lambda bi, k: (bi, c(k), 0))
    y_shape = jax.ShapeDtypeStruct((b, s, D_SSD), BF16)
    return pl.pallas_call(
        _ssd_kernel,
        grid=(b, nch),
        in_specs=[x_spec(fwd), x_spec(bwd), dt_spec(0, fwd), dt_spec(1, bwd),
                  full(tri), full(a_dir), full(dskip)],
        out_specs=[y_spec(fwd), y_spec(bwd)],
        out_shape=[y_shape, y_shape],
        scratch_shapes=[pltpu.VMEM((2, D_STATE, D_SSD), F32)],
        compiler_params=_params(("parallel", "arbitrary")),
        name="ssd_scan",
    )(xbcs, xbcs, dt, dt, tri, a_dir, dskip)


CM_ROWS = 32
CM_COLS = 512


def _mix_out_kernel(n_tiles, yf_ref, yb_ref, zs_ref, vprev_ref, v_ref, vnext_ref, x_ref, ng_ref, cw_ref,
                    cb_ref, lg_ref, lb_ref, w1_ref, w2_ref, pg_ref, gate_ref, scale_ref, shift_ref,
                    rw_ref, xo_ref, u_ref, lo_ref, sh_ref, cv_ref):
    _fill_ext(sh_ref.at[0], vprev_ref, v_ref, vnext_ref, n_tiles)
    shifted_rows = TM + 2 * BF16_ROWS - SUBLANES
    for r in range(1, SUBLANES):
        sh_ref[r, 0:shifted_rows, :] = sh_ref[0, r:r + shifted_rows, :]
    half = CONF_KERNEL // 2

    def row_block(i, carry):
        base = pl.multiple_of(i * CM_ROWS, CM_ROWS)
        for c0 in range(0, D_CONF, CM_COLS):
            cols = pl.ds(c0, CM_COLS)
            acc = jnp.broadcast_to(cb_ref[:, cols], (CM_ROWS, CM_COLS))
            for k in range(CONF_KERNEL):
                q, r = divmod(BF16_ROWS - half + k, SUBLANES)
                acc = acc + cw_ref[k:k + 1, cols] * sh_ref[r, pl.ds(base + q * SUBLANES, CM_ROWS), cols]
            cv_ref[pl.ds(base, CM_ROWS), cols] = acc
        return carry

    lax.fori_loop(0, TM // CM_ROWS, row_block, 0)

    cv = cv_ref[...]
    cc = cv - jnp.mean(cv, axis=-1, keepdims=True)
    ln = cc * lax.rsqrt(jnp.mean(cc * cc, axis=-1, keepdims=True) + EPS) * lg_ref[...] + lb_ref[...]
    vv = (ln * _sigmoid(ln)).astype(BF16)

    yz = (yf_ref[0].astype(F32) + yb_ref[0].astype(F32)) * zs_ref[0].astype(F32)
    y_ssd = yz * lax.rsqrt(jnp.mean(yz * yz, axis=-1, keepdims=True) + EPS) * ng_ref[...]
    m = _dot(y_ssd.astype(BF16), w1_ref[...]) + _dot(vv, w2_ref[...])
    m = m * lax.rsqrt(jnp.mean(m * m, axis=-1, keepdims=True) + EPS) * pg_ref[...]
    xn = x_ref[0] + gate_ref[0, 0] * m
    xo_ref[0] = xn
    u = xn * lax.rsqrt(jnp.mean(xn * xn, axis=-1, keepdims=True) + EPS) * scale_ref[0, 0] + shift_ref[0, 0]
    ub = u.astype(BF16)
    u_ref[0] = ub
    lo_ref[0] = _dot(ub, rw_ref[...])


def _mix_out(yf, yb, zs, v, xc, ng, cw, cb, lg, lb, w1, w2, pg, gate, scale, shift, rw):
    b, s, d = xc.shape
    n_tiles = s // TM
    prev, nxt = _halo_specs(D_CONF, n_tiles)
    full = lambda a: pl.BlockSpec(a.shape, lambda bi, t: (0,) * a.ndim)
    mod = pl.BlockSpec((1, 1, 1, d), lambda bi, t: (bi, _seg(t), 0, 0))
    tile = lambda width: pl.BlockSpec((1, TM, width), lambda bi, t: (bi, t, 0))
    return pl.pallas_call(
        functools.partial(_mix_out_kernel, n_tiles),
        grid=(b, n_tiles),
        in_specs=[tile(D_SSD), tile(D_SSD), tile(D_SSD), prev, tile(D_CONF), nxt, tile(d),
                  full(ng), full(cw), full(cb), full(lg), full(lb), full(w1), full(w2), full(pg),
                  mod, mod, mod, full(rw)],
        out_specs=[tile(d), tile(d), tile(LANES)],
        out_shape=[jax.ShapeDtypeStruct((b, s, d), F32),
                   jax.ShapeDtypeStruct((b, s, d), BF16),
                   jax.ShapeDtypeStruct((b, s, LANES), F32)],
        scratch_shapes=[pltpu.VMEM((SUBLANES, TM + 2 * BF16_ROWS, D_CONF), F32),
                        pltpu.VMEM((TM, D_CONF), F32)],
        compiler_params=_params(("parallel", "parallel")),
        name="mix_out",
    )(yf, yb, zs, v, v, v, xc, ng, cw, cb, lg, lb, w1, w2, pg, gate, scale, shift, rw)


FFN_COLS = 256


FFN_MAX_ROWS = 512


def _ffn_kernel(sel_ref, wg_ref, wu_ref, wd_ref, o_ref):
    nb, _, rows, d = sel_ref.shape
    s = sel_ref[:, 0].reshape(nb * rows, d)
    f = wg_ref.shape[-1]
    out = jnp.zeros((nb * rows, d), F32)
    for c0 in range(0, f, FFN_COLS):
        c1 = min(c0 + FFN_COLS, f)
        hg = _dot(s, wg_ref[0, 0, :, c0:c1].astype(BF16))
        hu = _dot(s, wu_ref[0, 0, :, c0:c1].astype(BF16))
        h = (hg * _sigmoid(hg) * hu).astype(BF16)
        out = out + _dot(h, wd_ref[0, 0, c0:c1, :].astype(BF16))
    o_ref[:, 0] = out.astype(BF16).reshape(nb, rows, d)


def _expert_ffn(sel, layer, wg, wu, wd):
    b, e, rows, d = sel.shape
    f = wg.shape[-1]
    nb = b if b * rows <= FFN_MAX_ROWS else 1
    return pl.pallas_call(
        _ffn_kernel,
        grid=(e, b // nb),
        in_specs=[pl.BlockSpec((nb, 1, rows, d), lambda ei, bi: (bi, ei, 0, 0)),
                  pl.BlockSpec((1, 1, d, f), lambda ei, bi: (layer, ei, 0, 0)),
                  pl.BlockSpec((1, 1, d, f), lambda ei, bi: (layer, ei, 0, 0)),
                  pl.BlockSpec((1, 1, f, d), lambda ei, bi: (layer, ei, 0, 0))],
        out_specs=pl.BlockSpec((nb, 1, rows, d), lambda ei, bi: (bi, ei, 0, 0)),
        out_shape=jax.ShapeDtypeStruct((b, e, rows, d), BF16),
        compiler_params=_params(("parallel", "arbitrary")),
        name="expert_ffn",
    )(sel, wg, wu, wd)


RT = 128
BISECT_BITS = 31


def _route_kernel(cap, row0, lg_ref, post_ref, posn_ref, gate_ref, starts_ref, aff_ref, afft_ref):
    n = lg_ref.shape[1] - row0
    nt = n // RT
    lane_n = lax.broadcasted_iota(jnp.int32, (RT, LANES), 1)
    real_n = lane_n < N_EXPERTS

    for i in range(nt):
        rows = slice(i * RT, (i + 1) * RT)
        lg = jnp.where(real_n, lg_ref[0, row0 + i * RT:row0 + (i + 1) * RT, :], -1e30)
        ex = jnp.exp(lg - jnp.max(lg, axis=-1, keepdims=True))
        aff = ex / jnp.sum(ex, axis=-1, keepdims=True)
        aff_ref[rows, :] = aff
        afft_ref[:, rows] = aff.T[0:N_EXPERTS, :]

    bits = lax.bitcast_convert_type(afft_ref[...], jnp.int32)

    def bisect(i, prefix):
        cand = prefix | jnp.left_shift(jnp.int32(1), BISECT_BITS - 1 - i)
        cnt = jnp.sum(jnp.where(bits >= cand, 1.0, 0.0), axis=1, keepdims=True)
        return jnp.where(cnt >= cap, cand, prefix)

    thr = lax.fori_loop(0, BISECT_BITS, bisect, jnp.zeros((N_EXPERTS, 1), jnp.int32))
    n_gt = jnp.sum(jnp.where(bits > thr, 1.0, 0.0), axis=1, keepdims=True)
    need = cap - n_gt

    r_i = lax.broadcasted_iota(jnp.int32, (RT, RT), 0)
    c_i = lax.broadcasted_iota(jnp.int32, (RT, RT), 1)
    upper = jnp.where(r_i <= c_i, 1.0, 0.0).astype(BF16)
    lower = jnp.where(c_i <= r_i, 1.0, 0.0).astype(BF16)
    lane_s = lax.broadcasted_iota(jnp.int32, (N_EXPERTS, LANES), 1)
    carry_eq = jnp.zeros((N_EXPERTS, 1), F32)
    carry_sel = jnp.zeros((N_EXPERTS, 1), F32)
    starts = jnp.zeros((N_EXPERTS, LANES), F32)
    for i in range(nt):
        cols = slice(i * RT, (i + 1) * RT)
        b_t = bits[:, cols]
        eq = jnp.where(b_t == thr, 1.0, 0.0)
        rank = _dot(eq.astype(BF16), upper) + carry_eq
        sel = jnp.where(jnp.logical_or(b_t > thr, jnp.logical_and(b_t == thr, rank <= need)), 1.0, 0.0)
        pos = _dot(sel.astype(BF16), upper) + carry_sel - sel
        post_ref[0, :, cols] = jnp.where(sel > 0.0, pos, -1.0).astype(jnp.int32)
        starts = jnp.where(lane_s == i, carry_sel, starts)
        carry_eq = carry_eq + jnp.sum(eq, axis=1, keepdims=True)
        carry_sel = carry_sel + jnp.sum(sel, axis=1, keepdims=True)
    starts_ref[0] = starts.astype(jnp.int32)

    sub_s = lax.broadcasted_iota(jnp.int32, (N_EXPERTS, LANES), 0)
    eye = sub_s == lane_s
    thr_row = jnp.sum(jnp.where(eye, lax.bitcast_convert_type(thr, F32), 0.0), axis=0, keepdims=True)
    need_row = jnp.sum(jnp.where(eye, need, 0.0), axis=0, keepdims=True)
    carry_eq = jnp.zeros((1, LANES), F32)
    carry_sel = jnp.zeros((1, LANES), F32)
    for i in range(nt):
        rows = slice(i * RT, (i + 1) * RT)
        aff = aff_ref[rows, :]
        eq_b = jnp.logical_and(aff == thr_row, real_n)
        eq = jnp.where(eq_b, 1.0, 0.0)
        rank = _dot(lower, eq.astype(BF16)) + carry_eq
        sel_b = jnp.logical_and(real_n, jnp.logical_or(aff > thr_row, jnp.logical_and(eq_b, rank <= need_row)))
        sel = jnp.where(sel_b, 1.0, 0.0)
        pos = _dot(lower, sel.astype(BF16)) + carry_sel - sel
        posn_ref[0, rows, :] = jnp.where(sel_b, pos, -1.0).astype(jnp.int32)
        gate_ref[0, rows, :] = jnp.where(sel_b, aff, 0.0)
        carry_eq = carry_eq + jnp.sum(eq, axis=0, keepdims=True)
        carry_sel = carry_sel + jnp.sum(sel, axis=0, keepdims=True)


def _route(logits, row0, n, cap):
    b = logits.shape[0]
    return pl.pallas_call(
        functools.partial(_route_kernel, cap, row0),
        grid=(b,),
        in_specs=[pl.BlockSpec((1, row0 + n, LANES), lambda bi: (bi, 0, 0))],
        out_specs=[pl.BlockSpec((1, N_EXPERTS, n), lambda bi: (bi, 0, 0)),
                   pl.BlockSpec((1, n, LANES), lambda bi: (bi, 0, 0)),
                   pl.BlockSpec((1, n, LANES), lambda bi: (bi, 0, 0)),
                   pl.BlockSpec((1, N_EXPERTS, LANES), lambda bi: (bi, 0, 0))],
        out_shape=[jax.ShapeDtypeStruct((b, N_EXPERTS, n), jnp.int32),
                   jax.ShapeDtypeStruct((b, n, LANES), jnp.int32),
                   jax.ShapeDtypeStruct((b, n, LANES), F32),
                   jax.ShapeDtypeStruct((b, N_EXPERTS, LANES), jnp.int32)],
        scratch_shapes=[pltpu.VMEM((n, LANES), F32), pltpu.VMEM((N_EXPERTS, n), F32)],
        compiler_params=_params(("parallel",)),
        name="route",
    )(logits)


GT = 256
MIN_ROWS = 128


def _window(start, align, width, rows):
    if width == rows:
        return 0
    a = jnp.minimum((start // align) * align, rows - width)
    return pl.multiple_of(a, align)


def _gather_kernel(nt_r, row0, starts_ref, post_ref, u_ref, sel_ref, acc_ref):
    bi, ei = pl.program_id(0), pl.program_id(1)
    rows = acc_ref.shape[0]
    n = u_ref.shape[1] - row0
    gt = min(GT, n)
    width = min(gt + BF16_ROWS, rows)
    acc_ref[...] = jnp.zeros_like(acc_ref)
    j = lax.broadcasted_iota(jnp.int32, (width, gt), 0)
    for i in range(n // gt):
        start = starts_ref[(bi * N_EXPERTS + ei) * nt_r + i * (gt // RT)]
        a = _window(start, BF16_ROWS, width, rows)
        p = post_ref[0, 0, :, i * gt:(i + 1) * gt]
        onehot = jnp.where(p - a == j, 1.0, 0.0).astype(BF16)
        acc_ref[pl.ds(a, width), :] += _dot(onehot, u_ref[0, row0 + i * gt:row0 + (i + 1) * gt, :])
    sel_ref[0, 0] = acc_ref[...].astype(BF16)


def _gather(starts, post, u, row0, rows):
    b, _, d = u.shape
    n = post.shape[-1]
    nt_r = n // RT
    post4 = post.reshape(b, N_EXPERTS, 1, n)
    return pl.pallas_call(
        functools.partial(_gather_kernel, nt_r, row0),
        grid_spec=pltpu.PrefetchScalarGridSpec(
            num_scalar_prefetch=1,
            grid=(b, N_EXPERTS),
            in_specs=[pl.BlockSpec((1, 1, 1, n), lambda bi, ei, st: (bi, ei, 0, 0)),
                      pl.BlockSpec((1, row0 + n, d), lambda bi, ei, st: (bi, 0, 0))],
            out_specs=pl.BlockSpec((1, 1, rows, d), lambda bi, ei, st: (bi, ei, 0, 0)),
            scratch_shapes=[pltpu.VMEM((rows, d), F32)]),
        out_shape=jax.ShapeDtypeStruct((b, N_EXPERTS, rows, d), BF16),
        compiler_params=_params(("parallel", "parallel")),
        name="moe_gather",
    )(starts, post4, u)


def _scatter_kernel(nt_r, starts_ref, out_ref, posn_ref, gate_ref, x_ref, pg_ref, g2_ref, xo_ref):
    bi, ti = pl.program_id(0), pl.program_id(1)
    rows = out_ref.shape[2]
    width = min(RT + LANES, rows)
    j = lax.broadcasted_iota(jnp.int32, (RT, width), 1)
    for sub in range(TM // RT):
        r0 = sub * RT
        y = jnp.zeros((RT, D_MODEL), F32)
        for e in range(N_EXPERTS):
            start = starts_ref[(bi * N_EXPERTS + e) * nt_r + ti * (TM // RT) + sub]
            a = _window(start, LANES, width, rows)
            p = posn_ref[0, r0:r0 + RT, e:e + 1]
            onehot = jnp.where(p - a == j, 1.0, 0.0).astype(BF16)
            y = y + gate_ref[0, r0:r0 + RT, e:e + 1] * _dot(onehot, out_ref[0, e, pl.ds(a, width), :])
        y = y * lax.rsqrt(jnp.mean(y * y, axis=-1, keepdims=True) + EPS) * pg_ref[...]
        xo_ref[0, r0:r0 + RT, :] = x_ref[0, r0:r0 + RT, :] + g2_ref[0, 0] * y


def _scatter(starts, out, posn, gate, xc, pg, g2, seg, in_tile0, out_tile0, out_rows, alias):
    b, n, _ = posn.shape
    d = xc.shape[-1]
    rows = out.shape[2]
    nt_r = n // RT
    return pl.pallas_call(
        functools.partial(_scatter_kernel, nt_r),
        grid_spec=pltpu.PrefetchScalarGridSpec(
            num_scalar_prefetch=1,
            grid=(b, n // TM),
            in_specs=[pl.BlockSpec((1, N_EXPERTS, rows, d), lambda bi, ti, st: (bi, 0, 0, 0)),
                      pl.BlockSpec((1, TM, LANES), lambda bi, ti, st: (bi, ti, 0)),
                      pl.BlockSpec((1, TM, LANES), lambda bi, ti, st: (bi, ti, 0)),
                      pl.BlockSpec((1, TM, d), lambda bi, ti, st: (bi, ti + in_tile0, 0)),
                      pl.BlockSpec((1, d), lambda bi, ti, st: (0, 0)),
                      pl.BlockSpec((1, 1, 1, d), lambda bi, ti, st: (bi, seg, 0, 0))],
            out_specs=pl.BlockSpec((1, TM, d), lambda bi, ti, st: (bi, ti + out_tile0, 0))),
        out_shape=jax.ShapeDtypeStruct((b, out_rows, d), F32),
        input_output_aliases={4: 0} if alias else {},
        compiler_params=_params(("parallel", "parallel")),
        name="moe_scatter",
    )(starts, out, posn, gate, xc, pg, g2)


def _moe_segment(xc, u, logits, row0, n, layer, wg, wu, wd, pg, g2, seg, out_tile0, out_rows, alias):
    cap = max(1, CAPACITY_FACTOR * n // N_EXPERTS)
    rows = max(cap, MIN_ROWS)
    post, posn, gate, starts = _route(logits, row0, n, cap)
    starts = starts[:, :, :n // RT].reshape(-1)
    sel = _gather(starts, post, u, row0, rows)
    out = _expert_ffn(sel, layer, wg, wu, wd)
    return _scatter(starts, out, posn, gate, xc, pg, g2, seg, row0 // TM, out_tile0, out_rows, alias)


def _pad_cols(w, n):
    return jnp.pad(w, ((0, 0), (0, n - w.shape[1])))


def kernel(x, c, ctx, c_ctx, ada_w, ada_b, mix_pre_g, w_in, conv_w, conv_b, dt_bias, a_log, d_skip,
           ssd_norm_g, cm_dw_w, cm_dw_b, cm_ln_g, cm_ln_b, w_out, mix_post_g, moe_pre_g, router_w,
           exp_w_gate, exp_w_up, exp_w_down, moe_post_g):
    b = x.shape[0]
    xc = jnp.concatenate([ctx, x], axis=1)
    row = lambda v: v.reshape(1, -1)

    li = lax.broadcasted_iota(jnp.int32, (CHUNK, CHUNK), 0)
    si = lax.broadcasted_iota(jnp.int32, (CHUNK, CHUNK), 1)
    tri = jnp.stack([(si <= li), (si >= li)]).astype(F32)

    cond = jnp.zeros((8, D_MODEL), F32).at[:b].set(c).at[b].set(c_ctx)
    cond = cond * _sigmoid(cond)

    for i in range(DEPTH):
        last = i == DEPTH - 1
        mods = _matmul(cond, ada_w[i]) + ada_b[i]
        mods = mods.reshape(8, 6, D_MODEL)
        both = lambda j: jnp.stack([jnp.broadcast_to(mods[b, j], (b, D_MODEL)), mods[:b, j]], axis=1)[:, :, None, :]
        sh1, sc1, g1, sh2, sc2, g2 = [both(j) for j in range(6)]

        wi = w_in[i].astype(BF16)
        wxbc = wi[:, :XBC_DIM]
        wdtf = _pad_cols(wi[:, XBC_DIM:XBC_DIM + N_HEADS], LANES)
        wdtb = _pad_cols(wi[:, XBC_DIM + N_HEADS:SSD_IN_COLS], LANES)
        wz = wi[:, SSD_IN_COLS:Z_END]
        wval = wi[:, Z_END:Z_END + D_CONF]
        wgate = wi[:, Z_END + D_CONF:]
        dtbf = _pad_cols(row(dt_bias[i, 0]), LANES)
        dtbb = _pad_cols(row(dt_bias[i, 1]), LANES)
        xbc, dt, zs, v = _mix_in(xc, mix_pre_g[i] * (1.0 + sc1), sh1, wxbc, wdtf, wdtb, wz, wval, wgate,
                                 dtbf, dtbb)

        xbcs = _conv_silu(xbc, conv_w[i], row(conv_b[i]))

        a_dir = _pad_cols(-jnp.exp(a_log[i]), LANES)[:, None, :]
        yf, yb = _ssd_scan(xbcs, dt, tri, a_dir, row(jnp.repeat(d_skip[i], HEAD_DIM)))

        wo = w_out[i].astype(BF16)
        rw = _pad_cols(router_w[i].astype(BF16), LANES)
        xc, u, logits = _mix_out(yf, yb, zs, v, xc, row(ssd_norm_g[i]), cm_dw_w[i], row(cm_dw_b[i]),
                                 row(cm_ln_g[i]), row(cm_ln_b[i]), wo[:D_SSD], wo[D_SSD:],
                                 row(mix_post_g[i]), g1, moe_pre_g[i] * (1.0 + sc2), sh2, rw)

        wg, wu, wd = exp_w_gate, exp_w_up, exp_w_down
        pg = row(moe_post_g[i])
        s_tot = xc.shape[1]
        n_lat = s_tot - CTX_LEN
        if not last:
            xc = _moe_segment(xc, u, logits, CTX_LEN, n_lat, i, wg, wu, wd, pg, g2, 1, CTX_LEN // TM, s_tot, True)
            xc = _moe_segment(xc, u, logits, 0, CTX_LEN, i, wg, wu, wd, pg, g2, 0, 0, s_tot, True)
        else:
            xc = _moe_segment(xc, u, logits, CTX_LEN, n_lat, i, wg, wu, wd, pg, g2, 1, 0, n_lat, False)
    return xc
```

```python
import functools

import jax
import jax.numpy as jnp
from jax import lax
from jax.experimental import pallas as pl
from jax.experimental.pallas import tpu as pltpu

D_MODEL = 1024
DEPTH = 2
CTX_LEN = 256
D_SSD = 1024
HEAD_DIM = 64
N_HEADS = D_SSD // HEAD_DIM
N_GROUPS = 4
HEADS_PER_GROUP = N_HEADS // N_GROUPS
D_STATE = 128
D_CONF = 1024
SSD_CONV = 5
CONF_KERNEL = 31
BC_DIM = N_GROUPS * D_STATE
XBC_DIM = D_SSD + 2 * BC_DIM
SSD_IN_COLS = XBC_DIM + 2 * N_HEADS
Z_END = SSD_IN_COLS + D_SSD
N_EXPERTS = 16
CAPACITY_FACTOR = 2
EPS = 1e-6

LANES = 128
SUBLANES = 8
BF16_ROWS = 16
TM = 256
CHUNK = 128
VMEM_LIMIT_BYTES = 56 * 1024 * 1024

F32 = jnp.float32
BF16 = jnp.bfloat16


def _dot(a, b):
    return jnp.dot(a, b, preferred_element_type=F32)


def _sigmoid(x):
    return 1.0 / (1.0 + jnp.exp(-x))


def _params(sem):
    return pltpu.CompilerParams(dimension_semantics=sem, vmem_limit_bytes=VMEM_LIMIT_BYTES)


def _mm_kernel(a_ref, w_ref, o_ref):
    o_ref[...] = _dot(a_ref[...].astype(BF16), w_ref[...].astype(BF16))


def _matmul(a, w, tn=512):
    m, k = a.shape
    n = w.shape[1]
    return pl.pallas_call(
        _mm_kernel,
        grid=(n // tn,),
        in_specs=[pl.BlockSpec((m, k), lambda j: (0, 0)),
                  pl.BlockSpec((k, tn), lambda j: (0, j))],
        out_specs=pl.BlockSpec((m, tn), lambda j: (0, j)),
        out_shape=jax.ShapeDtypeStruct((m, n), F32),
        compiler_params=_params(("parallel",)),
        name="adaln_matmul",
    )(a, w)


def _mix_in_kernel(head_ref, body_ref, scale_ref, shift_ref, wxbc_ref, wdtf_ref, wdtb_ref, wz_ref, wval_ref,
                   wgate_ref, dtbf_ref, dtbb_ref, xbc_ref, dt_ref, zs_ref, v_ref):
    x = _stream_tile(head_ref, body_ref)
    ms = jnp.mean(x * x, axis=-1, keepdims=True)
    u = x * lax.rsqrt(ms + EPS) * scale_ref[0, 0] + shift_ref[0, 0]
    ub = u.astype(BF16)
    xbc_ref[0] = _dot(ub, wxbc_ref[...]).astype(BF16)

    def softplus(t):
        return jnp.maximum(t, 0.0) + jnp.log(1.0 + jnp.exp(-jnp.abs(t)))

    dt_ref[0, 0] = softplus(_dot(ub, wdtf_ref[...]) + dtbf_ref[...])
    dt_ref[1, 0] = softplus(_dot(ub, wdtb_ref[...]) + dtbb_ref[...])
    z = _dot(ub, wz_ref[...])
    zs_ref[0] = (z * _sigmoid(z)).astype(BF16)
    val = _dot(ub, wval_ref[...])
    gate = _dot(ub, wgate_ref[...])
    v_ref[0] = (val * _sigmoid(gate)).astype(BF16)


def _seg(t):
    return jnp.minimum(t, 1)


def _stream_specs(body_tile0, d):
    head = pl.BlockSpec((1, TM, d), lambda bi, t: (bi, 0, 0))
    body = pl.BlockSpec((1, TM, d), lambda bi, t: (bi, jnp.maximum(t - 1 + body_tile0, 0), 0))
    return head, body


def _stream_tile(head_ref, body_ref):
    return jnp.where(pl.program_id(1) == 0, head_ref[0], body_ref[0])


def _mix_in(head, body, body_tile0, s, scale, shift, wxbc, wdtf, wdtb, wz, wval, wgate, dtbf, dtbb):
    b, _, d = head.shape
    head_spec, body_spec = _stream_specs(body_tile0, d)
    full = lambda shape: pl.BlockSpec(shape, lambda bi, t: (0,) * len(shape))
    mod = pl.BlockSpec((1, 1, 1, d), lambda bi, t: (bi, _seg(t), 0, 0))
    return pl.pallas_call(
        _mix_in_kernel,
        grid=(b, s // TM),
        in_specs=[head_spec, body_spec, mod, mod,
                  full(wxbc.shape), full(wdtf.shape), full(wdtb.shape), full(wz.shape),
                  full(wval.shape), full(wgate.shape), full(dtbf.shape), full(dtbb.shape)],
        out_specs=[pl.BlockSpec((1, TM, XBC_DIM), lambda bi, t: (bi, t, 0)),
                   pl.BlockSpec((2, 1, TM, LANES), lambda bi, t: (0, bi, t, 0)),
                   pl.BlockSpec((1, TM, D_SSD), lambda bi, t: (bi, t, 0)),
                   pl.BlockSpec((1, TM, D_CONF), lambda bi, t: (bi, t, 0))],
        out_shape=[jax.ShapeDtypeStruct((b, s, XBC_DIM), BF16),
                   jax.ShapeDtypeStruct((2, b, s, LANES), F32),
                   jax.ShapeDtypeStruct((b, s, D_SSD), BF16),
                   jax.ShapeDtypeStruct((b, s, D_CONF), BF16)],
        compiler_params=_params(("parallel", "parallel")),
        name="mix_in",
    )(head, body, scale, shift, wxbc, wdtf, wdtb, wz, wval, wgate, dtbf, dtbb)


def _halo_specs(width, n_tiles):
    per_tile = TM // BF16_ROWS
    n_blocks = n_tiles * per_tile
    prev = pl.BlockSpec((1, BF16_ROWS, width),
                        lambda bi, t: (bi, jnp.maximum(t * per_tile - 1, 0), 0))
    nxt = pl.BlockSpec((1, BF16_ROWS, width),
                       lambda bi, t: (bi, jnp.minimum((t + 1) * per_tile, n_blocks - 1), 0))
    return prev, nxt


def _fill_ext(ext_ref, prev_ref, cur_ref, next_ref, n_tiles):
    t = pl.program_id(1)
    prev_ok = jnp.where(t >= 2, 1.0, 0.0)
    next_ok = jnp.where(jnp.logical_and(t >= 1, t < n_tiles - 1), 1.0, 0.0)
    ext_ref[0:BF16_ROWS, :] = prev_ref[0].astype(F32) * prev_ok
    ext_ref[BF16_ROWS:BF16_ROWS + TM, :] = cur_ref[0].astype(F32)
    ext_ref[BF16_ROWS + TM:, :] = next_ref[0].astype(F32) * next_ok


CONV_ROWS = 128
CONV_COLS = 512


def _conv_silu_kernel(n_tiles, prev_ref, cur_ref, next_ref, w_ref, b_ref, o_ref, ext_ref):
    t = pl.program_id(1)
    prev_ok = jnp.where(t >= 2, 1.0, 0.0)
    next_ok = jnp.where(jnp.logical_and(t >= 1, t < n_tiles - 1), 1.0, 0.0)
    ext_ref[0:BF16_ROWS, :] = (prev_ref[0].astype(F32) * prev_ok).astype(BF16)
    ext_ref[BF16_ROWS:BF16_ROWS + TM, :] = cur_ref[0]
    ext_ref[BF16_ROWS + TM:, :] = (next_ref[0].astype(F32) * next_ok).astype(BF16)

    half = SSD_CONV // 2
    win_rows = CONV_ROWS + 2 * BF16_ROWS
    r = lax.broadcasted_iota(jnp.int32, ((SSD_CONV - 1) * CONV_ROWS, win_rows), 0)
    c = lax.broadcasted_iota(jnp.int32, ((SSD_CONV - 1) * CONV_ROWS, win_rows), 1)
    tap = jnp.right_shift(r, CONV_ROWS.bit_length() - 1)
    off = tap - half + jnp.where(tap >= half, 1, 0)
    shift = jnp.where(c == (r & (CONV_ROWS - 1)) + BF16_ROWS + off, 1.0, 0.0).astype(BF16)
    side_taps = [k for k in range(SSD_CONV) if k != half]

    for r0 in range(0, TM, CONV_ROWS):
        for c0 in range(0, XBC_DIM, CONV_COLS):
            cols = pl.ds(c0, CONV_COLS)
            shifted = _dot(shift, ext_ref[r0:r0 + win_rows, cols])
            centre = ext_ref[r0 + BF16_ROWS:r0 + BF16_ROWS + CONV_ROWS, cols].astype(F32)
            acc = b_ref[:, cols] + w_ref[half:half + 1, cols] * centre
            for i, k in enumerate(side_taps):
                acc = acc + w_ref[k:k + 1, cols] * shifted[i * CONV_ROWS:(i + 1) * CONV_ROWS, :]
            o_ref[0, r0:r0 + CONV_ROWS, cols] = (acc * _sigmoid(acc)).astype(BF16)


def _conv_silu(xbc, w, bias):
    b, s, c = xbc.shape
    n_tiles = s // TM
    prev, nxt = _halo_specs(c, n_tiles)
    return pl.pallas_call(
        functools.partial(_conv_silu_kernel, n_tiles),
        grid=(b, n_tiles),
        in_specs=[prev, pl.BlockSpec((1, TM, c), lambda bi, t: (bi, t, 0)), nxt,
                  pl.BlockSpec(w.shape, lambda bi, t: (0, 0)),
                  pl.BlockSpec(bias.shape, lambda bi, t: (0, 0))],
        out_specs=pl.BlockSpec((1, TM, c), lambda bi, t: (bi, t, 0)),
        out_shape=jax.ShapeDtypeStruct((b, s, c), BF16),
        scratch_shapes=[pltpu.VMEM((TM + 2 * BF16_ROWS, c), BF16)],
        compiler_params=_params(("parallel", "parallel")),
        name="conv_silu",
    )(xbc, xbc, xbc, w, bias)


def _split3(v):
    hi = v.astype(BF16)
    r1 = v - hi.astype(F32)
    mid = r1.astype(BF16)
    lo = (r1 - mid.astype(F32)).astype(BF16)
    return hi, mid, lo


def _ssd_kernel(xf_ref, xb_ref, dtf_ref, dtb_ref, tri_ref, a_ref, dskip_ref, yf_ref, yb_ref, state_ref):
    @pl.when(pl.program_id(1) == 0)
    def _():
        state_ref[...] = jnp.zeros_like(state_ref)

    _ssd_chunk(xf_ref, dtf_ref, tri_ref.at[0], a_ref.at[0], dskip_ref, yf_ref, state_ref.at[0])
    _ssd_chunk(xb_ref, dtb_ref, tri_ref.at[1], a_ref.at[1], None, yb_ref, state_ref.at[1])


def _ssd_chunk(x_ref, dt_ref, tri_ref, a_ref, dskip_ref, y_ref, state_ref):
    dt = dt_ref[0, 0]
    tri = tri_ref[...]
    da = dt * a_ref[...]
    tri_b = tri.astype(BF16)
    hi, mid, lo = _split3(da)
    cum = _dot(tri_b, hi) + _dot(tri_b, mid) + _dot(tri_b, lo)
    tot = jnp.sum(da, axis=0, keepdims=True)
    w = dt * jnp.exp(tot - cum)
    dec = jnp.exp(tot)
    cum_t = cum.T
    dt_t = dt.T
    w_t = w.T
    valid = tri > 0.0
    lane = lax.broadcasted_iota(jnp.int32, (CHUNK, 2 * HEAD_DIM), 1)
    first = lane < HEAD_DIM

    for g in range(N_GROUPS):
        b_g = x_ref[0, :, D_SSD + g * D_STATE:D_SSD + (g + 1) * D_STATE]
        c_g = x_ref[0, :, D_SSD + BC_DIM + g * D_STATE:D_SSD + BC_DIM + (g + 1) * D_STATE]
        b_gt = b_g.astype(F32).T
        scores = lax.dot_general(c_g, b_g, (((1,), (1,)), ((), ())), preferred_element_type=F32)
        gcols = slice(g * HEADS_PER_GROUP * HEAD_DIM, (g + 1) * HEADS_PER_GROUP * HEAD_DIM)
        y_off = _dot(c_g, state_ref[:, gcols].astype(BF16))
        for pair in range(HEADS_PER_GROUP // 2):
            h0 = g * HEADS_PER_GROUP + 2 * pair
            cols = slice(h0 * HEAD_DIM, (h0 + 2) * HEAD_DIM)
            x_p = x_ref[0, :, cols]
            y_heads, upd_heads, off_heads = [], [], []
            for h in (h0, h0 + 1):
                cum_l = jnp.broadcast_to(cum[:, h:h + 1], (CHUNK, CHUNK))
                seg = cum_l - cum_t[h:h + 1, :]
                decay = jnp.exp(jnp.where(valid, seg, -1e30))
                m = (scores * decay * dt_t[h:h + 1, :]).astype(BF16)
                y_heads.append(_dot(m, x_p))
                bw = (b_gt * w_t[h:h + 1, :]).astype(BF16)
                upd_heads.append(_dot(bw, x_p))
                off_heads.append(jnp.exp(cum_l))
            y_diag = jnp.where(first, y_heads[0], y_heads[1])
            upd = jnp.where(first, upd_heads[0], upd_heads[1])
            off_scale = jnp.where(first, off_heads[0], off_heads[1])
            pcols = slice(2 * pair * HEAD_DIM, (2 * pair + 2) * HEAD_DIM)
            y = y_diag + off_scale * y_off[:, pcols]
            if dskip_ref is not None:
                y = y + dskip_ref[:, cols] * x_p.astype(F32)
            y_ref[0, :, cols] = y.astype(BF16)
            dec_p = jnp.where(first[0:1, :], dec[:, h0:h0 + 1], dec[:, h0 + 1:h0 + 2])
            state_ref[:, cols] = dec_p * state_ref[:, cols] + upd


def _ssd_scan(xbcs, dt, tri, a_dir, dskip):
    b, s, _ = xbcs.shape
    nch = s // CHUNK
    nctx = CTX_LEN // CHUNK

    fwd = lambda k: k
    bwd = lambda k: jnp.where(k < nctx, nctx - 1 - k, nch + nctx - 1 - k)
    full = lambda a: pl.BlockSpec(a.shape, lambda bi, k: (0,) * a.ndim)
    x_spec = lambda c: pl.BlockSpec((1, CHUNK, XBC_DIM), lambda bi, k: (bi, c(k), 0))
    dt_spec = lambda d, c: pl.BlockSpec((1, 1, CHUNK, LANES), lambda bi, k: (d, bi, c(k), 0))
    y_spec = lambda c: pl.BlockSpec((1, CHUNK, D_SSD), lambda bi, k: (bi, c(k), 0))
    y_shape = jax.ShapeDtypeStruct((b, s, D_SSD), BF16)
    return pl.pallas_call(
        _ssd_kernel,
        grid=(b, nch),
        in_specs=[x_spec(fwd), x_spec(bwd), dt_spec(0, fwd), dt_spec(1, bwd),
                  full(tri), full(a_dir), full(dskip)],
        out_specs=[y_spec(fwd), y_spec(bwd)],
        out_shape=[y_shape, y_shape],
        scratch_shapes=[pltpu.VMEM((2, D_STATE, D_SSD), F32)],
        compiler_params=_params(("parallel", "arbitrary")),
        name="ssd_scan",
    )(xbcs, xbcs, dt, dt, tri, a_dir, dskip)


CM_ROWS = 32
CM_COLS = 512


def _mix_out_kernel(n_tiles, yf_ref, yb_ref, zs_ref, vprev_ref, v_ref, vnext_ref, head_ref, body_ref, ng_ref, cw_ref,
                    cb_ref, lg_ref, lb_ref, w1_ref, w2_ref, pg_ref, gate_ref, scale_ref, shift_ref,
                    rw_ref, xo_ref, u_ref, lo_ref, sh_ref, cv_ref):
    _fill_ext(sh_ref.at[0], vprev_ref, v_ref, vnext_ref, n_tiles)
    shifted_rows = TM + 2 * BF16_ROWS - SUBLANES
    for r in range(1, SUBLANES):
        sh_ref[r, 0:shifted_rows, :] = sh_ref[0, r:r + shifted_rows, :]
    half = CONF_KERNEL // 2

    def row_block(i, carry):
        base = pl.multiple_of(i * CM_ROWS, CM_ROWS)
        for c0 in range(0, D_CONF, CM_COLS):
            cols = pl.ds(c0, CM_COLS)
            acc = jnp.broadcast_to(cb_ref[:, cols], (CM_ROWS, CM_COLS))
            for k in range(CONF_KERNEL):
                q, r = divmod(BF16_ROWS - half + k, SUBLANES)
                acc = acc + cw_ref[k:k + 1, cols] * sh_ref[r, pl.ds(base + q * SUBLANES, CM_ROWS), cols]
            cv_ref[pl.ds(base, CM_ROWS), cols] = acc
        return carry

    lax.fori_loop(0, TM // CM_ROWS, row_block, 0)

    cv = cv_ref[...]
    cc = cv - jnp.mean(cv, axis=-1, keepdims=True)
    ln = cc * lax.rsqrt(jnp.mean(cc * cc, axis=-1, keepdims=True) + EPS) * lg_ref[...] + lb_ref[...]
    vv = (ln * _sigmoid(ln)).astype(BF16)

    yz = (yf_ref[0].astype(F32) + yb_ref[0].astype(F32)) * zs_ref[0].astype(F32)
    y_ssd = yz * lax.rsqrt(jnp.mean(yz * yz, axis=-1, keepdims=True) + EPS) * ng_ref[...]
    m = _dot(y_ssd.astype(BF16), w1_ref[...]) + _dot(vv, w2_ref[...])
    m = m * lax.rsqrt(jnp.mean(m * m, axis=-1, keepdims=True) + EPS) * pg_ref[...]
    xn = _stream_tile(head_ref, body_ref) + gate_ref[0, 0] * m
    xo_ref[0] = xn
    u = xn * lax.rsqrt(jnp.mean(xn * xn, axis=-1, keepdims=True) + EPS) * scale_ref[0, 0] + shift_ref[0, 0]
    ub = u.astype(BF16)
    u_ref[0] = ub
    lo_ref[0] = _dot(ub, rw_ref[...])


def _mix_out(yf, yb, zs, v, head, body, body_tile0, ng, cw, cb, lg, lb, w1, w2, pg, gate, scale, shift, rw):
    b, s, _ = zs.shape
    d = head.shape[-1]
    n_tiles = s // TM
    head_spec, body_spec = _stream_specs(body_tile0, d)
    prev, nxt = _halo_specs(D_CONF, n_tiles)
    full = lambda a: pl.BlockSpec(a.shape, lambda bi, t: (0,) * a.ndim)
    mod = pl.BlockSpec((1, 1, 1, d), lambda bi, t: (bi, _seg(t), 0, 0))
    tile = lambda width: pl.BlockSpec((1, TM, width), lambda bi, t: (bi, t, 0))
    return pl.pallas_call(
        functools.partial(_mix_out_kernel, n_tiles),
        grid=(b, n_tiles),
        in_specs=[tile(D_SSD), tile(D_SSD), tile(D_SSD), prev, tile(D_CONF), nxt, head_spec, body_spec,
                  full(ng), full(cw), full(cb), full(lg), full(lb), full(w1), full(w2), full(pg),
                  mod, mod, mod, full(rw)],
        out_specs=[tile(d), tile(d), tile(LANES)],
        out_shape=[jax.ShapeDtypeStruct((b, s, d), F32),
                   jax.ShapeDtypeStruct((b, s, d), BF16),
                   jax.ShapeDtypeStruct((b, s, LANES), F32)],
        scratch_shapes=[pltpu.VMEM((SUBLANES, TM + 2 * BF16_ROWS, D_CONF), F32),
                        pltpu.VMEM((TM, D_CONF), F32)],
        compiler_params=_params(("parallel", "parallel")),
        name="mix_out",
    )(yf, yb, zs, v, v, v, head, body, ng, cw, cb, lg, lb, w1, w2, pg, gate, scale, shift, rw)


FFN_COLS = 256


FFN_MAX_ROWS = 1024


def _ffn_kernel(sel_ref, wg_ref, wu_ref, wd_ref, o_ref):
    nb, _, rows, d = sel_ref.shape
    s = sel_ref[:, 0].reshape(nb * rows, d)
    f = wg_ref.shape[-1]
    out = jnp.zeros((nb * rows, d), F32)
    for c0 in range(0, f, FFN_COLS):
        c1 = min(c0 + FFN_COLS, f)
        hg = _dot(s, wg_ref[0, 0, :, c0:c1].astype(BF16))
        hu = _dot(s, wu_ref[0, 0, :, c0:c1].astype(BF16))
        h = (hg * _sigmoid(hg) * hu).astype(BF16)
        out = out + _dot(h, wd_ref[0, 0, c0:c1, :].astype(BF16))
    o_ref[:, 0] = out.astype(BF16).reshape(nb, rows, d)


def _expert_ffn(sel, layer, wg, wu, wd):
    b, e, rows, d = sel.shape
    f = wg.shape[-1]
    nb = max(k for k in range(1, b + 1) if b % k == 0 and k * rows <= max(rows, FFN_MAX_ROWS))
    return pl.pallas_call(
        _ffn_kernel,
        grid=(e, b // nb),
        in_specs=[pl.BlockSpec((nb, 1, rows, d), lambda ei, bi: (bi, ei, 0, 0)),
                  pl.BlockSpec((1, 1, d, f), lambda ei, bi: (layer, ei, 0, 0)),
                  pl.BlockSpec((1, 1, d, f), lambda ei, bi: (layer, ei, 0, 0)),
                  pl.BlockSpec((1, 1, f, d), lambda ei, bi: (layer, ei, 0, 0))],
        out_specs=pl.BlockSpec((nb, 1, rows, d), lambda ei, bi: (bi, ei, 0, 0)),
        out_shape=jax.ShapeDtypeStruct((b, e, rows, d), BF16),
        compiler_params=_params(("parallel", "arbitrary")),
        name="expert_ffn",
    )(sel, wg, wu, wd)


RT = 128
BISECT_BITS = 31


def _route_kernel(cap, row0, lg_ref, post_ref, posn_ref, gate_ref, starts_ref, aff_ref, afft_ref):
    n = lg_ref.shape[1] - row0
    nt = n // RT
    lane_n = lax.broadcasted_iota(jnp.int32, (RT, LANES), 1)
    real_n = lane_n < N_EXPERTS

    for i in range(nt):
        rows = slice(i * RT, (i + 1) * RT)
        lg = jnp.where(real_n, lg_ref[0, row0 + i * RT:row0 + (i + 1) * RT, :], -1e30)
        ex = jnp.exp(lg - jnp.max(lg, axis=-1, keepdims=True))
        aff = ex / jnp.sum(ex, axis=-1, keepdims=True)
        aff_ref[rows, :] = aff
        afft_ref[:, rows] = aff.T[0:N_EXPERTS, :]

    bits = lax.bitcast_convert_type(afft_ref[...], jnp.int32)

    def bisect(i, prefix):
        cand = prefix | jnp.left_shift(jnp.int32(1), BISECT_BITS - 1 - i)
        cnt = jnp.sum(jnp.where(bits >= cand, 1.0, 0.0), axis=1, keepdims=True)
        return jnp.where(cnt >= cap, cand, prefix)

    thr = lax.fori_loop(0, BISECT_BITS, bisect, jnp.zeros((N_EXPERTS, 1), jnp.int32))
    n_gt = jnp.sum(jnp.where(bits > thr, 1.0, 0.0), axis=1, keepdims=True)
    need = cap - n_gt

    r_i = lax.broadcasted_iota(jnp.int32, (RT, RT), 0)
    c_i = lax.broadcasted_iota(jnp.int32, (RT, RT), 1)
    upper = jnp.where(r_i <= c_i, 1.0, 0.0).astype(BF16)
    lower = jnp.where(c_i <= r_i, 1.0, 0.0).astype(BF16)
    lane_s = lax.broadcasted_iota(jnp.int32, (N_EXPERTS, LANES), 1)
    carry_eq = jnp.zeros((N_EXPERTS, 1), F32)
    carry_sel = jnp.zeros((N_EXPERTS, 1), F32)
    starts = jnp.zeros((N_EXPERTS, LANES), F32)
    for i in range(nt):
        cols = slice(i * RT, (i + 1) * RT)
        b_t = bits[:, cols]
        eq = jnp.where(b_t == thr, 1.0, 0.0)
        rank = _dot(eq.astype(BF16), upper) + carry_eq
        sel = jnp.where(jnp.logical_or(b_t > thr, jnp.logical_and(b_t == thr, rank <= need)), 1.0, 0.0)
        pos = _dot(sel.astype(BF16), upper) + carry_sel - sel
        post_ref[0, :, cols] = jnp.where(sel > 0.0, pos, -1.0).astype(jnp.int32)
        starts = jnp.where(lane_s == i, carry_sel, starts)
        carry_eq = carry_eq + jnp.sum(eq, axis=1, keepdims=True)
        carry_sel = carry_sel + jnp.sum(sel, axis=1, keepdims=True)
    starts_ref[0] = starts.astype(jnp.int32)

    sub_s = lax.broadcasted_iota(jnp.int32, (N_EXPERTS, LANES), 0)
    eye = sub_s == lane_s
    thr_row = jnp.sum(jnp.where(eye, lax.bitcast_convert_type(thr, F32), 0.0), axis=0, keepdims=True)
    need_row = jnp.sum(jnp.where(eye, need, 0.0), axis=0, keepdims=True)
    carry_eq = jnp.zeros((1, LANES), F32)
    carry_sel = jnp.zeros((1, LANES), F32)
    for i in range(nt):
        rows = slice(i * RT, (i + 1) * RT)
        aff = aff_ref[rows, :]
        eq_b = jnp.logical_and(aff == thr_row, real_n)
        eq = jnp.where(eq_b, 1.0, 0.0)
        rank = _dot(lower, eq.astype(BF16)) + carry_eq
        sel_b = jnp.logical_and(real_n, jnp.logical_or(aff > thr_row, jnp.logical_and(eq_b, rank <= need_row)))
        sel = jnp.where(sel_b, 1.0, 0.0)
        pos = _dot(lower, sel.astype(BF16)) + carry_sel - sel
        posn_ref[0, rows, :] = jnp.where(sel_b, pos, -1.0).astype(jnp.int32)
        gate_ref[0, rows, :] = jnp.where(sel_b, aff, 0.0)
        carry_eq = carry_eq + jnp.sum(eq, axis=0, keepdims=True)
        carry_sel = carry_sel + jnp.sum(sel, axis=0, keepdims=True)


def _route(logits, row0, n, cap):
    b = logits.shape[0]
    return pl.pallas_call(
        functools.partial(_route_kernel, cap, row0),
        grid=(b,),
        in_specs=[pl.BlockSpec((1, row0 + n, LANES), lambda bi: (bi, 0, 0))],
        out_specs=[pl.BlockSpec((1, N_EXPERTS, n), lambda bi: (bi, 0, 0)),
                   pl.BlockSpec((1, n, LANES), lambda bi: (bi, 0, 0)),
                   pl.BlockSpec((1, n, LANES), lambda bi: (bi, 0, 0)),
                   pl.BlockSpec((1, N_EXPERTS, LANES), lambda bi: (bi, 0, 0))],
        out_shape=[jax.ShapeDtypeStruct((b, N_EXPERTS, n), jnp.int32),
                   jax.ShapeDtypeStruct((b, n, LANES), jnp.int32),
                   jax.ShapeDtypeStruct((b, n, LANES), F32),
                   jax.ShapeDtypeStruct((b, N_EXPERTS, LANES), jnp.int32)],
        scratch_shapes=[pltpu.VMEM((n, LANES), F32), pltpu.VMEM((N_EXPERTS, n), F32)],
        compiler_params=_params(("parallel",)),
        name="route",
    )(logits)


GT = 256
MIN_ROWS = 128


def _window(start, align, width, rows):
    if width == rows:
        return 0
    a = jnp.minimum((start // align) * align, rows - width)
    return pl.multiple_of(a, align)


def _gather_kernel(nt_r, row0, starts_ref, post_ref, u_ref, sel_ref, acc_ref):
    bi, ei = pl.program_id(0), pl.program_id(1)
    rows = acc_ref.shape[0]
    n = u_ref.shape[1] - row0
    gt = min(GT, n)
    width = min(gt + BF16_ROWS, rows)
    acc_ref[...] = jnp.zeros_like(acc_ref)
    j = lax.broadcasted_iota(jnp.int32, (width, gt), 0)
    for i in range(n // gt):
        start = starts_ref[(bi * N_EXPERTS + ei) * nt_r + i * (gt // RT)]
        a = _window(start, BF16_ROWS, width, rows)
        p = post_ref[0, 0, :, i * gt:(i + 1) * gt]
        onehot = jnp.where(p - a == j, 1.0, 0.0).astype(BF16)
        acc_ref[pl.ds(a, width), :] += _dot(onehot, u_ref[0, row0 + i * gt:row0 + (i + 1) * gt, :])
    sel_ref[0, 0] = acc_ref[...].astype(BF16)


def _gather(starts, post, u, row0, rows):
    b, _, d = u.shape
    n = post.shape[-1]
    nt_r = n // RT
    post4 = post.reshape(b, N_EXPERTS, 1, n)
    return pl.pallas_call(
        functools.partial(_gather_kernel, nt_r, row0),
        grid_spec=pltpu.PrefetchScalarGridSpec(
            num_scalar_prefetch=1,
            grid=(b, N_EXPERTS),
            in_specs=[pl.BlockSpec((1, 1, 1, n), lambda bi, ei, st: (bi, ei, 0, 0)),
                      pl.BlockSpec((1, row0 + n, d), lambda bi, ei, st: (bi, 0, 0))],
            out_specs=pl.BlockSpec((1, 1, rows, d), lambda bi, ei, st: (bi, ei, 0, 0)),
            scratch_shapes=[pltpu.VMEM((rows, d), F32)]),
        out_shape=jax.ShapeDtypeStruct((b, N_EXPERTS, rows, d), BF16),
        compiler_params=_params(("parallel", "parallel")),
        name="moe_gather",
    )(starts, post4, u)


def _scatter_kernel(nt_r, starts_ref, out_ref, posn_ref, gate_ref, x_ref, pg_ref, g2_ref, xo_ref):
    bi, ti = pl.program_id(0), pl.program_id(1)
    rows = out_ref.shape[2]
    width = min(RT + LANES, rows)
    j = lax.broadcasted_iota(jnp.int32, (RT, width), 1)
    for sub in range(TM // RT):
        r0 = sub * RT
        y = jnp.zeros((RT, D_MODEL), F32)
        for e in range(N_EXPERTS):
            start = starts_ref[(bi * N_EXPERTS + e) * nt_r + ti * (TM // RT) + sub]
            a = _window(start, LANES, width, rows)
            p = posn_ref[0, r0:r0 + RT, e:e + 1]
            onehot = jnp.where(p - a == j, 1.0, 0.0).astype(BF16)
            y = y + gate_ref[0, r0:r0 + RT, e:e + 1] * _dot(onehot, out_ref[0, e, pl.ds(a, width), :])
        y = y * lax.rsqrt(jnp.mean(y * y, axis=-1, keepdims=True) + EPS) * pg_ref[...]
        xo_ref[0, r0:r0 + RT, :] = x_ref[0, r0:r0 + RT, :] + g2_ref[0, 0] * y


def _scatter(starts, out, posn, gate, xc, pg, g2, seg, in_tile0, out_tile0, out_rows, alias):
    b, n, _ = posn.shape
    d = xc.shape[-1]
    rows = out.shape[2]
    nt_r = n // RT
    return pl.pallas_call(
        functools.partial(_scatter_kernel, nt_r),
        grid_spec=pltpu.PrefetchScalarGridSpec(
            num_scalar_prefetch=1,
            grid=(b, n // TM),
            in_specs=[pl.BlockSpec((1, N_EXPERTS, rows, d), lambda bi, ti, st: (bi, 0, 0, 0)),
                      pl.BlockSpec((1, TM, LANES), lambda bi, ti, st: (bi, ti, 0)),
                      pl.BlockSpec((1, TM, LANES), lambda bi, ti, st: (bi, ti, 0)),
                      pl.BlockSpec((1, TM, d), lambda bi, ti, st: (bi, ti + in_tile0, 0)),
                      pl.BlockSpec((1, d), lambda bi, ti, st: (0, 0)),
                      pl.BlockSpec((1, 1, 1, d), lambda bi, ti, st: (bi, seg, 0, 0))],
            out_specs=pl.BlockSpec((1, TM, d), lambda bi, ti, st: (bi, ti + out_tile0, 0))),
        out_shape=jax.ShapeDtypeStruct((b, out_rows, d), F32),
        input_output_aliases={4: 0} if alias else {},
        compiler_params=_params(("parallel", "parallel")),
        name="moe_scatter",
    )(starts, out, posn, gate, xc, pg, g2)


def _moe_segment(xc, u, logits, row0, n, layer, wg, wu, wd, pg, g2, seg, out_tile0, out_rows, alias):
    cap = max(1, CAPACITY_FACTOR * n // N_EXPERTS)
    rows = max(cap, MIN_ROWS)
    post, posn, gate, starts = _route(logits, row0, n, cap)
    starts = starts[:, :, :n // RT].reshape(-1)
    sel = _gather(starts, post, u, row0, rows)
    out = _expert_ffn(sel, layer, wg, wu, wd)
    return _scatter(starts, out, posn, gate, xc, pg, g2, seg, row0 // TM, out_tile0, out_rows, alias)


def _pad_cols(w, n):
    return jnp.pad(w, ((0, 0), (0, n - w.shape[1])))


def kernel(x, c, ctx, c_ctx, ada_w, ada_b, mix_pre_g, w_in, conv_w, conv_b, dt_bias, a_log, d_skip,
           ssd_norm_g, cm_dw_w, cm_dw_b, cm_ln_g, cm_ln_b, w_out, mix_post_g, moe_pre_g, router_w,
           exp_w_gate, exp_w_up, exp_w_down, moe_post_g):
    b = x.shape[0]
    s_tot = CTX_LEN + x.shape[1]
    head, body, body_tile0 = ctx, x, 0
    row = lambda v: v.reshape(1, -1)

    li = lax.broadcasted_iota(jnp.int32, (CHUNK, CHUNK), 0)
    si = lax.broadcasted_iota(jnp.int32, (CHUNK, CHUNK), 1)
    tri = jnp.stack([(si <= li), (si >= li)]).astype(F32)

    cond = jnp.zeros((8, D_MODEL), F32).at[:b].set(c).at[b].set(c_ctx)
    cond = cond * _sigmoid(cond)

    for i in range(DEPTH):
        last = i == DEPTH - 1
        mods = _matmul(cond, ada_w[i]) + ada_b[i]
        mods = mods.reshape(8, 6, D_MODEL)
        both = lambda j: jnp.stack([jnp.broadcast_to(mods[b, j], (b, D_MODEL)), mods[:b, j]], axis=1)[:, :, None, :]
        sh1, sc1, g1, sh2, sc2, g2 = [both(j) for j in range(6)]

        wi = w_in[i].astype(BF16)
        wxbc = wi[:, :XBC_DIM]
        wdtf = _pad_cols(wi[:, XBC_DIM:XBC_DIM + N_HEADS], LANES)
        wdtb = _pad_cols(wi[:, XBC_DIM + N_HEADS:SSD_IN_COLS], LANES)
        wz = wi[:, SSD_IN_COLS:Z_END]
        wval = wi[:, Z_END:Z_END + D_CONF]
        wgate = wi[:, Z_END + D_CONF:]
        dtbf = _pad_cols(row(dt_bias[i, 0]), LANES)
        dtbb = _pad_cols(row(dt_bias[i, 1]), LANES)
        xbc, dt, zs, v = _mix_in(head, body, body_tile0, s_tot, mix_pre_g[i] * (1.0 + sc1), sh1, wxbc, wdtf, wdtb, wz, wval, wgate,
                                 dtbf, dtbb)

        xbcs = _conv_silu(xbc, conv_w[i], row(conv_b[i]))

        a_dir = _pad_cols(-jnp.exp(a_log[i]), LANES)[:, None, :]
        yf, yb = _ssd_scan(xbcs, dt, tri, a_dir, row(jnp.repeat(d_skip[i], HEAD_DIM)))

        wo = w_out[i].astype(BF16)
        rw = _pad_cols(router_w[i].astype(BF16), LANES)
        xc, u, logits = _mix_out(yf, yb, zs, v, head, body, body_tile0, row(ssd_norm_g[i]), cm_dw_w[i], row(cm_dw_b[i]),
                                 row(cm_ln_g[i]), row(cm_ln_b[i]), wo[:D_SSD], wo[D_SSD:],
                                 row(mix_post_g[i]), g1, moe_pre_g[i] * (1.0 + sc2), sh2, rw)

        wg, wu, wd = exp_w_gate, exp_w_up, exp_w_down
        pg = row(moe_post_g[i])
        n_lat = s_tot - CTX_LEN
        if not last:
            xc = _moe_segment(xc, u, logits, CTX_LEN, n_lat, i, wg, wu, wd, pg, g2, 1, CTX_LEN // TM, s_tot, True)
            xc = _moe_segment(xc, u, logits, 0, CTX_LEN, i, wg, wu, wd, pg, g2, 0, 0, s_tot, True)
            head, body, body_tile0 = xc, xc, CTX_LEN // TM
        else:
            xc = _moe_segment(xc, u, logits, CTX_LEN, n_lat, i, wg, wu, wd, pg, g2, 1, 0, n_lat, False)
    return xc
```

```python
import functools

import jax
import jax.numpy as jnp
from jax import lax
from jax.experimental import pallas as pl
from jax.experimental.pallas import tpu as pltpu

D_MODEL = 1024
DEPTH = 2
CTX_LEN = 256
D_SSD = 1024
HEAD_DIM = 64
N_HEADS = D_SSD // HEAD_DIM
N_GROUPS = 4
HEADS_PER_GROUP = N_HEADS // N_GROUPS
D_STATE = 128
D_CONF = 1024
SSD_CONV = 5
CONF_KERNEL = 31
BC_DIM = N_GROUPS * D_STATE
XBC_DIM = D_SSD + 2 * BC_DIM
SSD_IN_COLS = XBC_DIM + 2 * N_HEADS
Z_END = SSD_IN_COLS + D_SSD
N_EXPERTS = 16
CAPACITY_FACTOR = 2
EPS = 1e-6

LANES = 128
SUBLANES = 8
BF16_ROWS = 16
TM = 256
CHUNK = 128
VMEM_LIMIT_BYTES = 56 * 1024 * 1024

F32 = jnp.float32
BF16 = jnp.bfloat16


def _dot(a, b):
    return jnp.dot(a, b, preferred_element_type=F32)


def _sigmoid(x):
    return 1.0 / (1.0 + jnp.exp(-x))


def _params(sem):
    return pltpu.CompilerParams(dimension_semantics=sem, vmem_limit_bytes=VMEM_LIMIT_BYTES)


def _mm_kernel(a_ref, w_ref, o_ref):
    o_ref[...] = _dot(a_ref[...].astype(BF16), w_ref[...].astype(BF16))


def _matmul(a, w, tn=512):
    m, k = a.shape
    n = w.shape[1]
    return pl.pallas_call(
        _mm_kernel,
        grid=(n // tn,),
        in_specs=[pl.BlockSpec((m, k), lambda j: (0, 0)),
                  pl.BlockSpec((k, tn), lambda j: (0, j))],
        out_specs=pl.BlockSpec((m, tn), lambda j: (0, j)),
        out_shape=jax.ShapeDtypeStruct((m, n), F32),
        compiler_params=_params(("parallel",)),
        name="adaln_matmul",
    )(a, w)


def _mix_in_kernel(head_ref, body_ref, scale_ref, shift_ref, wxbc_ref, wdtf_ref, wdtb_ref, wz_ref, wval_ref,
                   wgate_ref, dtbf_ref, dtbb_ref, xbc_ref, dt_ref, zs_ref, v_ref):
    x = _stream_tile(head_ref, body_ref)
    ms = jnp.mean(x * x, axis=-1, keepdims=True)
    u = x * lax.rsqrt(ms + EPS) * scale_ref[0, 0] + shift_ref[0, 0]
    ub = u.astype(BF16)
    xbc_ref[0] = _dot(ub, wxbc_ref[...]).astype(BF16)

    def softplus(t):
        return jnp.maximum(t, 0.0) + jnp.log(1.0 + jnp.exp(-jnp.abs(t)))

    dt_ref[0, 0] = softplus(_dot(ub, wdtf_ref[...]) + dtbf_ref[...])
    dt_ref[1, 0] = softplus(_dot(ub, wdtb_ref[...]) + dtbb_ref[...])
    z = _dot(ub, wz_ref[...])
    zs_ref[0] = (z * _sigmoid(z)).astype(BF16)
    val = _dot(ub, wval_ref[...])
    gate = _dot(ub, wgate_ref[...])
    v_ref[0] = (val * _sigmoid(gate)).astype(BF16)


def _seg(t):
    return jnp.minimum(t, 1)


def _stream_specs(body_tile0, d):
    head = pl.BlockSpec((1, TM, d), lambda bi, t: (bi, 0, 0))
    body = pl.BlockSpec((1, TM, d), lambda bi, t: (bi, jnp.maximum(t - 1 + body_tile0, 0), 0))
    return head, body


def _stream_tile(head_ref, body_ref):
    return jnp.where(pl.program_id(1) == 0, head_ref[0], body_ref[0])


def _mix_in(head, body, body_tile0, s, scale, shift, wxbc, wdtf, wdtb, wz, wval, wgate, dtbf, dtbb):
    b, _, d = head.shape
    head_spec, body_spec = _stream_specs(body_tile0, d)
    full = lambda shape: pl.BlockSpec(shape, lambda bi, t: (0,) * len(shape))
    mod = pl.BlockSpec((1, 1, 1, d), lambda bi, t: (bi, _seg(t), 0, 0))
    return pl.pallas_call(
        _mix_in_kernel,
        grid=(b, s // TM),
        in_specs=[head_spec, body_spec, mod, mod,
                  full(wxbc.shape), full(wdtf.shape), full(wdtb.shape), full(wz.shape),
                  full(wval.shape), full(wgate.shape), full(dtbf.shape), full(dtbb.shape)],
        out_specs=[pl.BlockSpec((1, TM, XBC_DIM), lambda bi, t: (bi, t, 0)),
                   pl.BlockSpec((2, 1, TM, LANES), lambda bi, t: (0, bi, t, 0)),
                   pl.BlockSpec((1, TM, D_SSD), lambda bi, t: (bi, t, 0)),
                   pl.BlockSpec((1, TM, D_CONF), lambda bi, t: (bi, t, 0))],
        out_shape=[jax.ShapeDtypeStruct((b, s, XBC_DIM), BF16),
                   jax.ShapeDtypeStruct((2, b, s, LANES), F32),
                   jax.ShapeDtypeStruct((b, s, D_SSD), BF16),
                   jax.ShapeDtypeStruct((b, s, D_CONF), BF16)],
        compiler_params=_params(("parallel", "parallel")),
        name="mix_in",
    )(head, body, scale, shift, wxbc, wdtf, wdtb, wz, wval, wgate, dtbf, dtbb)


def _halo_specs(width, n_tiles):
    per_tile = TM // BF16_ROWS
    n_blocks = n_tiles * per_tile
    prev = pl.BlockSpec((1, BF16_ROWS, width),
                        lambda bi, t: (bi, jnp.maximum(t * per_tile - 1, 0), 0))
    nxt = pl.BlockSpec((1, BF16_ROWS, width),
                       lambda bi, t: (bi, jnp.minimum((t + 1) * per_tile, n_blocks - 1), 0))
    return prev, nxt


def _fill_ext(ext_ref, prev_ref, cur_ref, next_ref, n_tiles):
    t = pl.program_id(1)
    prev_ok = jnp.where(t >= 2, 1.0, 0.0)
    next_ok = jnp.where(jnp.logical_and(t >= 1, t < n_tiles - 1), 1.0, 0.0)
    ext_ref[0:BF16_ROWS, :] = prev_ref[0].astype(F32) * prev_ok
    ext_ref[BF16_ROWS:BF16_ROWS + TM, :] = cur_ref[0].astype(F32)
    ext_ref[BF16_ROWS + TM:, :] = next_ref[0].astype(F32) * next_ok


CONV_ROWS = 128
CONV_COLS = 512


def _conv_silu_kernel(n_tiles, prev_ref, cur_ref, next_ref, w_ref, b_ref, o_ref, ext_ref):
    t = pl.program_id(1)
    prev_ok = jnp.where(t >= 2, 1.0, 0.0)
    next_ok = jnp.where(jnp.logical_and(t >= 1, t < n_tiles - 1), 1.0, 0.0)
    ext_ref[0:BF16_ROWS, :] = (prev_ref[0].astype(F32) * prev_ok).astype(BF16)
    ext_ref[BF16_ROWS:BF16_ROWS + TM, :] = cur_ref[0]
    ext_ref[BF16_ROWS + TM:, :] = (next_ref[0].astype(F32) * next_ok).astype(BF16)

    half = SSD_CONV // 2
    win_rows = CONV_ROWS + 2 * BF16_ROWS
    r = lax.broadcasted_iota(jnp.int32, ((SSD_CONV - 1) * CONV_ROWS, win_rows), 0)
    c = lax.broadcasted_iota(jnp.int32, ((SSD_CONV - 1) * CONV_ROWS, win_rows), 1)
    tap = jnp.right_shift(r, CONV_ROWS.bit_length() - 1)
    off = tap - half + jnp.where(tap >= half, 1, 0)
    shift = jnp.where(c == (r & (CONV_ROWS - 1)) + BF16_ROWS + off, 1.0, 0.0).astype(BF16)
    side_taps = [k for k in range(SSD_CONV) if k != half]

    for r0 in range(0, TM, CONV_ROWS):
        for c0 in range(0, XBC_DIM, CONV_COLS):
            cols = pl.ds(c0, CONV_COLS)
            shifted = _dot(shift, ext_ref[r0:r0 + win_rows, cols])
            centre = ext_ref[r0 + BF16_ROWS:r0 + BF16_ROWS + CONV_ROWS, cols].astype(F32)
            acc = b_ref[:, cols] + w_ref[half:half + 1, cols] * centre
            for i, k in enumerate(side_taps):
                acc = acc + w_ref[k:k + 1, cols] * shifted[i * CONV_ROWS:(i + 1) * CONV_ROWS, :]
            o_ref[0, r0:r0 + CONV_ROWS, cols] = (acc * _sigmoid(acc)).astype(BF16)


def _conv_silu(xbc, w, bias):
    b, s, c = xbc.shape
    n_tiles = s // TM
    prev, nxt = _halo_specs(c, n_tiles)
    return pl.pallas_call(
        functools.partial(_conv_silu_kernel, n_tiles),
        grid=(b, n_tiles),
        in_specs=[prev, pl.BlockSpec((1, TM, c), lambda bi, t: (bi, t, 0)), nxt,
                  pl.BlockSpec(w.shape, lambda bi, t: (0, 0)),
                  pl.BlockSpec(bias.shape, lambda bi, t: (0, 0))],
        out_specs=pl.BlockSpec((1, TM, c), lambda bi, t: (bi, t, 0)),
        out_shape=jax.ShapeDtypeStruct((b, s, c), BF16),
        scratch_shapes=[pltpu.VMEM((TM + 2 * BF16_ROWS, c), BF16)],
        compiler_params=_params(("parallel", "parallel")),
        name="conv_silu",
    )(xbc, xbc, xbc, w, bias)


def _split3(v):
    hi = v.astype(BF16)
    r1 = v - hi.astype(F32)
    mid = r1.astype(BF16)
    lo = (r1 - mid.astype(F32)).astype(BF16)
    return hi, mid, lo


def _ssd_kernel(xf_ref, xb_ref, dtf_ref, dtb_ref, tri_ref, a_ref, dskip_ref, yf_ref, yb_ref, state_ref):
    @pl.when(pl.program_id(1) == 0)
    def _():
        state_ref[...] = jnp.zeros_like(state_ref)

    _ssd_chunk(xf_ref, dtf_ref, tri_ref.at[0], a_ref.at[0], dskip_ref, yf_ref, state_ref.at[0])
    _ssd_chunk(xb_ref, dtb_ref, tri_ref.at[1], a_ref.at[1], None, yb_ref, state_ref.at[1])


def _ssd_chunk(x_ref, dt_ref, tri_ref, a_ref, dskip_ref, y_ref, state_ref):
    dt = dt_ref[0, 0]
    tri = tri_ref[...]
    da = dt * a_ref[...]
    tri_b = tri.astype(BF16)
    hi, mid, lo = _split3(da)
    cum = _dot(tri_b, hi) + _dot(tri_b, mid) + _dot(tri_b, lo)
    tot = jnp.sum(da, axis=0, keepdims=True)
    w = dt * jnp.exp(tot - cum)
    dec = jnp.exp(tot)
    cum_t = cum.T
    dt_t = dt.T
    w_t = w.T
    valid = tri > 0.0
    lane = lax.broadcasted_iota(jnp.int32, (CHUNK, 2 * HEAD_DIM), 1)
    first = lane < HEAD_DIM

    for g in range(N_GROUPS):
        b_g = x_ref[0, :, D_SSD + g * D_STATE:D_SSD + (g + 1) * D_STATE]
        c_g = x_ref[0, :, D_SSD + BC_DIM + g * D_STATE:D_SSD + BC_DIM + (g + 1) * D_STATE]
        b_gt = b_g.astype(F32).T
        scores = lax.dot_general(c_g, b_g, (((1,), (1,)), ((), ())), preferred_element_type=F32)
        gcols = slice(g * HEADS_PER_GROUP * HEAD_DIM, (g + 1) * HEADS_PER_GROUP * HEAD_DIM)
        y_off = _dot(c_g, state_ref[:, gcols].astype(BF16))
        for pair in range(HEADS_PER_GROUP // 2):
            h0 = g * HEADS_PER_GROUP + 2 * pair
            cols = slice(h0 * HEAD_DIM, (h0 + 2) * HEAD_DIM)
            x_p = x_ref[0, :, cols]
            y_heads, upd_heads, off_heads = [], [], []
            for h in (h0, h0 + 1):
                cum_l = jnp.broadcast_to(cum[:, h:h + 1], (CHUNK, CHUNK))
                seg = cum_l - cum_t[h:h + 1, :]
                decay = jnp.exp(jnp.where(valid, seg, -1e30))
                m = (scores * decay * dt_t[h:h + 1, :]).astype(BF16)
                y_heads.append(_dot(m, x_p))
                bw = (b_gt * w_t[h:h + 1, :]).astype(BF16)
                upd_heads.append(_dot(bw, x_p))
                off_heads.append(jnp.exp(cum_l))
            y_diag = jnp.where(first, y_heads[0], y_heads[1])
            upd = jnp.where(first, upd_heads[0], upd_heads[1])
            off_scale = jnp.where(first, off_heads[0], off_heads[1])
            pcols = slice(2 * pair * HEAD_DIM, (2 * pair + 2) * HEAD_DIM)
            y = y_diag + off_scale * y_off[:, pcols]
            if dskip_ref is not None:
                y = y + dskip_ref[:, cols] * x_p.astype(F32)
            y_ref[0, :, cols] = y.astype(BF16)
            dec_p = jnp.where(first[0:1, :], dec[:, h0:h0 + 1], dec[:, h0 + 1:h0 + 2])
            state_ref[:, cols] = dec_p * state_ref[:, cols] + upd


def _ssd_scan(xbcs, dt, tri, a_dir, dskip):
    b, s, _ = xbcs.shape
    nch = s // CHUNK
    nctx = CTX_LEN // CHUNK

    fwd = lambda k: k
    bwd = lambda k: jnp.where(k < nctx, nctx - 1 - k, nch + nctx - 1 - k)
    full = lambda a: pl.BlockSpec(a.shape, lambda bi, k: (0,) * a.ndim)
    x_spec = lambda c: pl.BlockSpec((1, CHUNK, XBC_DIM), lambda bi, k: (bi, c(k), 0))
    dt_spec = lambda d, c: pl.BlockSpec((1, 1, CHUNK, LANES), lambda bi, k: (d, bi, c(k), 0))
    y_spec = lambda c: pl.BlockSpec((1, CHUNK, D_SSD), lambda bi, k: (bi, c(k), 0))
    y_shape = jax.ShapeDtypeStruct((b, s, D_SSD), BF16)
    return pl.pallas_call(
        _ssd_kernel,
        grid=(b, nch),
        in_specs=[x_spec(fwd), x_spec(bwd), dt_spec(0, fwd), dt_spec(1, bwd),
                  full(tri), full(a_dir), full(dskip)],
        out_specs=[y_spec(fwd), y_spec(bwd)],
        out_shape=[y_shape, y_shape],
        scratch_shapes=[pltpu.VMEM((2, D_STATE, D_SSD), F32)],
        compiler_params=_params(("parallel", "arbitrary")),
        name="ssd_scan",
    )(xbcs, xbcs, dt, dt, tri, a_dir, dskip)


CM_ROWS = 32
CM_COLS = 512


def _mix_out_kernel(n_tiles, yf_ref, yb_ref, zs_ref, vprev_ref, v_ref, vnext_ref, head_ref, body_ref, ng_ref, cw_ref,
                    cb_ref, lg_ref, lb_ref, w1_ref, w2_ref, pg_ref, gate_ref, scale_ref, shift_ref,
                    rw_ref, xo_ref, u_ref, lo_ref, sh_ref, cv_ref):
    _fill_ext(sh_ref.at[0], vprev_ref, v_ref, vnext_ref, n_tiles)
    shifted_rows = TM + 2 * BF16_ROWS - SUBLANES
    for r in range(1, SUBLANES):
        sh_ref[r, 0:shifted_rows, :] = sh_ref[0, r:r + shifted_rows, :]
    half = CONF_KERNEL // 2

    def row_block(i, carry):
        base = pl.multiple_of(i * CM_ROWS, CM_ROWS)
        for c0 in range(0, D_CONF, CM_COLS):
            cols = pl.ds(c0, CM_COLS)
            accs = [cb_ref[:, cols] for _ in range(CM_ROWS // SUBLANES)]
            for k in range(CONF_KERNEL):
                q, r = divmod(BF16_ROWS - half + k, SUBLANES)
                w8 = cw_ref[k, :, cols]
                for g in range(CM_ROWS // SUBLANES):
                    accs[g] = accs[g] + w8 * sh_ref[r, pl.ds(base + (q + g) * SUBLANES, SUBLANES), cols]
            for g in range(CM_ROWS // SUBLANES):
                cv_ref[pl.ds(base + g * SUBLANES, SUBLANES), cols] = accs[g]
        return carry

    lax.fori_loop(0, TM // CM_ROWS, row_block, 0)

    cv = cv_ref[...]
    cc = cv - jnp.mean(cv, axis=-1, keepdims=True)
    ln = cc * lax.rsqrt(jnp.mean(cc * cc, axis=-1, keepdims=True) + EPS) * lg_ref[...] + lb_ref[...]
    vv = (ln * _sigmoid(ln)).astype(BF16)

    yz = (yf_ref[0].astype(F32) + yb_ref[0].astype(F32)) * zs_ref[0].astype(F32)
    y_ssd = yz * lax.rsqrt(jnp.mean(yz * yz, axis=-1, keepdims=True) + EPS) * ng_ref[...]
    m = _dot(y_ssd.astype(BF16), w1_ref[...]) + _dot(vv, w2_ref[...])
    m = m * lax.rsqrt(jnp.mean(m * m, axis=-1, keepdims=True) + EPS) * pg_ref[...]
    xn = _stream_tile(head_ref, body_ref) + gate_ref[0, 0] * m
    xo_ref[0] = xn
    u = xn * lax.rsqrt(jnp.mean(xn * xn, axis=-1, keepdims=True) + EPS) * scale_ref[0, 0] + shift_ref[0, 0]
    ub = u.astype(BF16)
    u_ref[0] = ub
    lo_ref[0] = _dot(ub, rw_ref[...])


def _mix_out(yf, yb, zs, v, head, body, body_tile0, ng, cw, cb, lg, lb, w1, w2, pg, gate, scale, shift, rw):
    b, s, _ = zs.shape
    d = head.shape[-1]
    n_tiles = s // TM
    head_spec, body_spec = _stream_specs(body_tile0, d)
    prev, nxt = _halo_specs(D_CONF, n_tiles)
    full = lambda a: pl.BlockSpec(a.shape, lambda bi, t: (0,) * a.ndim)
    mod = pl.BlockSpec((1, 1, 1, d), lambda bi, t: (bi, _seg(t), 0, 0))
    tile = lambda width: pl.BlockSpec((1, TM, width), lambda bi, t: (bi, t, 0))
    return pl.pallas_call(
        functools.partial(_mix_out_kernel, n_tiles),
        grid=(b, n_tiles),
        in_specs=[tile(D_SSD), tile(D_SSD), tile(D_SSD), prev, tile(D_CONF), nxt, head_spec, body_spec,
                  full(ng), full(cw), full(cb), full(lg), full(lb), full(w1), full(w2), full(pg),
                  mod, mod, mod, full(rw)],
        out_specs=[tile(d), tile(d), tile(LANES)],
        out_shape=[jax.ShapeDtypeStruct((b, s, d), F32),
                   jax.ShapeDtypeStruct((b, s, d), BF16),
                   jax.ShapeDtypeStruct((b, s, LANES), F32)],
        scratch_shapes=[pltpu.VMEM((SUBLANES, TM + 2 * BF16_ROWS, D_CONF), F32),
                        pltpu.VMEM((TM, D_CONF), F32)],
        compiler_params=_params(("parallel", "parallel")),
        name="mix_out",
    )(yf, yb, zs, v, v, v, head, body, ng, cw, cb, lg, lb, w1, w2, pg, gate, scale, shift, rw)


FFN_COLS = 256


FFN_MAX_ROWS = 1024


def _ffn_kernel(sel_ref, wg_ref, wu_ref, wd_ref, o_ref):
    nb, _, rows, d = sel_ref.shape
    s = sel_ref[:, 0].reshape(nb * rows, d)
    f = wg_ref.shape[-1]
    out = jnp.zeros((nb * rows, d), F32)
    for c0 in range(0, f, FFN_COLS):
        c1 = min(c0 + FFN_COLS, f)
        hg = _dot(s, wg_ref[0, 0, :, c0:c1].astype(BF16))
        hu = _dot(s, wu_ref[0, 0, :, c0:c1].astype(BF16))
        h = (hg * _sigmoid(hg) * hu).astype(BF16)
        out = out + _dot(h, wd_ref[0, 0, c0:c1, :].astype(BF16))
    o_ref[:, 0] = out.astype(BF16).reshape(nb, rows, d)


def _expert_ffn(sel, layer, wg, wu, wd):
    b, e, rows, d = sel.shape
    f = wg.shape[-1]
    nb = max(k for k in range(1, b + 1) if b % k == 0 and k * rows <= max(rows, FFN_MAX_ROWS))
    return pl.pallas_call(
        _ffn_kernel,
        grid=(e, b // nb),
        in_specs=[pl.BlockSpec((nb, 1, rows, d), lambda ei, bi: (bi, ei, 0, 0)),
                  pl.BlockSpec((1, 1, d, f), lambda ei, bi: (layer, ei, 0, 0)),
                  pl.BlockSpec((1, 1, d, f), lambda ei, bi: (layer, ei, 0, 0)),
                  pl.BlockSpec((1, 1, f, d), lambda ei, bi: (layer, ei, 0, 0))],
        out_specs=pl.BlockSpec((nb, 1, rows, d), lambda ei, bi: (bi, ei, 0, 0)),
        out_shape=jax.ShapeDtypeStruct((b, e, rows, d), BF16),
        compiler_params=_params(("parallel", "arbitrary")),
        name="expert_ffn",
    )(sel, wg, wu, wd)


RT = 128
BISECT_BITS = 31


def _route_kernel(cap, row0, lg_ref, post_ref, posn_ref, gate_ref, starts_ref, aff_ref, afft_ref):
    n = lg_ref.shape[1] - row0
    nt = n // RT
    lane_n = lax.broadcasted_iota(jnp.int32, (RT, LANES), 1)
    real_n = lane_n < N_EXPERTS

    for i in range(nt):
        rows = slice(i * RT, (i + 1) * RT)
        lg = jnp.where(real_n, lg_ref[0, row0 + i * RT:row0 + (i + 1) * RT, :], -1e30)
        ex = jnp.exp(lg - jnp.max(lg, axis=-1, keepdims=True))
        aff = ex / jnp.sum(ex, axis=-1, keepdims=True)
        aff_ref[rows, :] = aff
        afft_ref[:, rows] = aff.T[0:N_EXPERTS, :]

    bits = lax.bitcast_convert_type(afft_ref[...], jnp.int32)

    def bisect(i, prefix):
        cand = prefix | jnp.left_shift(jnp.int32(1), BISECT_BITS - 1 - i)
        cnt = jnp.sum(jnp.where(bits >= cand, 1.0, 0.0), axis=1, keepdims=True)
        return jnp.where(cnt >= cap, cand, prefix)

    thr = lax.fori_loop(0, BISECT_BITS, bisect, jnp.zeros((N_EXPERTS, 1), jnp.int32))
    n_gt = jnp.sum(jnp.where(bits > thr, 1.0, 0.0), axis=1, keepdims=True)
    need = cap - n_gt

    r_i = lax.broadcasted_iota(jnp.int32, (RT, RT), 0)
    c_i = lax.broadcasted_iota(jnp.int32, (RT, RT), 1)
    upper = jnp.where(r_i <= c_i, 1.0, 0.0).astype(BF16)
    lower = jnp.where(c_i <= r_i, 1.0, 0.0).astype(BF16)
    lane_s = lax.broadcasted_iota(jnp.int32, (N_EXPERTS, LANES), 1)
    carry_eq = jnp.zeros((N_EXPERTS, 1), F32)
    carry_sel = jnp.zeros((N_EXPERTS, 1), F32)
    starts = jnp.zeros((N_EXPERTS, LANES), F32)
    for i in range(nt):
        cols = slice(i * RT, (i + 1) * RT)
        b_t = bits[:, cols]
        eq = jnp.where(b_t == thr, 1.0, 0.0)
        rank = _dot(eq.astype(BF16), upper) + carry_eq
        sel = jnp.where(jnp.logical_or(b_t > thr, jnp.logical_and(b_t == thr, rank <= need)), 1.0, 0.0)
        pos = _dot(sel.astype(BF16), upper) + carry_sel - sel
        post_ref[0, :, cols] = jnp.where(sel > 0.0, pos, -1.0).astype(jnp.int32)
        starts = jnp.where(lane_s == i, carry_sel, starts)
        carry_eq = carry_eq + jnp.sum(eq, axis=1, keepdims=True)
        carry_sel = carry_sel + jnp.sum(sel, axis=1, keepdims=True)
    starts = jnp.where(lane_s == nt, carry_sel, starts)
    starts_ref[0] = starts.astype(jnp.int32)

    sub_s = lax.broadcasted_iota(jnp.int32, (N_EXPERTS, LANES), 0)
    eye = sub_s == lane_s
    thr_row = jnp.sum(jnp.where(eye, lax.bitcast_convert_type(thr, F32), 0.0), axis=0, keepdims=True)
    need_row = jnp.sum(jnp.where(eye, need, 0.0), axis=0, keepdims=True)
    carry_eq = jnp.zeros((1, LANES), F32)
    carry_sel = jnp.zeros((1, LANES), F32)
    for i in range(nt):
        rows = slice(i * RT, (i + 1) * RT)
        aff = aff_ref[rows, :]
        eq_b = jnp.logical_and(aff == thr_row, real_n)
        eq = jnp.where(eq_b, 1.0, 0.0)
        rank = _dot(lower, eq.astype(BF16)) + carry_eq
        sel_b = jnp.logical_and(real_n, jnp.logical_or(aff > thr_row, jnp.logical_and(eq_b, rank <= need_row)))
        sel = jnp.where(sel_b, 1.0, 0.0)
        pos = _dot(lower, sel.astype(BF16)) + carry_sel - sel
        posn_ref[0, rows, :] = jnp.where(sel_b, pos, -1.0).astype(jnp.int32)
        gate_ref[0, rows, :] = jnp.where(sel_b, aff, 0.0)
        carry_eq = carry_eq + jnp.sum(eq, axis=0, keepdims=True)
        carry_sel = carry_sel + jnp.sum(sel, axis=0, keepdims=True)


def _route(logits, row0, n, cap):
    b = logits.shape[0]
    return pl.pallas_call(
        functools.partial(_route_kernel, cap, row0),
        grid=(b,),
        in_specs=[pl.BlockSpec((1, row0 + n, LANES), lambda bi: (bi, 0, 0))],
        out_specs=[pl.BlockSpec((1, N_EXPERTS, n), lambda bi: (bi, 0, 0)),
                   pl.BlockSpec((1, n, LANES), lambda bi: (bi, 0, 0)),
                   pl.BlockSpec((1, n, LANES), lambda bi: (bi, 0, 0)),
                   pl.BlockSpec((1, N_EXPERTS, LANES), lambda bi: (bi, 0, 0))],
        out_shape=[jax.ShapeDtypeStruct((b, N_EXPERTS, n), jnp.int32),
                   jax.ShapeDtypeStruct((b, n, LANES), jnp.int32),
                   jax.ShapeDtypeStruct((b, n, LANES), F32),
                   jax.ShapeDtypeStruct((b, N_EXPERTS, LANES), jnp.int32)],
        scratch_shapes=[pltpu.VMEM((n, LANES), F32), pltpu.VMEM((N_EXPERTS, n), F32)],
        compiler_params=_params(("parallel",)),
        name="route",
    )(logits)


GT = 256
GATHER_NARROW = 80
MIN_ROWS = 128


def _window(start, align, width, rows):
    if width == rows:
        return 0
    a = jnp.minimum((start // align) * align, rows - width)
    return pl.multiple_of(a, align)


def _gather_kernel(nt_r, row0, starts_ref, post_ref, u_ref, sel_ref, acc_ref):
    bi, ei = pl.program_id(0), pl.program_id(1)
    rows = acc_ref.shape[0]
    n = u_ref.shape[1] - row0
    gt = min(GT, n)
    wide = min(gt + BF16_ROWS, rows)
    narrow = min(GATHER_NARROW, rows)
    acc_ref[...] = jnp.zeros_like(acc_ref)
    for i in range(n // gt):
        idx = (bi * N_EXPERTS + ei) * (nt_r + 1) + i * (gt // RT)
        start = starts_ref[idx]
        count = starts_ref[idx + gt // RT] - start

        def accumulate(width, i=i, start=start):
            a = _window(start, BF16_ROWS, width, rows)
            j = lax.broadcasted_iota(jnp.int32, (width, gt), 0)
            p = post_ref[0, 0, :, i * gt:(i + 1) * gt]
            onehot = jnp.where(p - a == j, 1.0, 0.0).astype(BF16)
            acc_ref[pl.ds(a, width), :] += _dot(onehot, u_ref[0, row0 + i * gt:row0 + (i + 1) * gt, :])

        if narrow == wide:
            accumulate(wide)
        else:
            fits = start - _window(start, BF16_ROWS, narrow, rows) + count <= narrow
            pl.when(fits)(functools.partial(accumulate, narrow))
            pl.when(jnp.logical_not(fits))(functools.partial(accumulate, wide))
    sel_ref[0, 0] = acc_ref[...].astype(BF16)


def _gather(starts, post, u, row0, rows):
    b, _, d = u.shape
    n = post.shape[-1]
    nt_r = n // RT
    post4 = post.reshape(b, N_EXPERTS, 1, n)
    return pl.pallas_call(
        functools.partial(_gather_kernel, nt_r, row0),
        grid_spec=pltpu.PrefetchScalarGridSpec(
            num_scalar_prefetch=1,
            grid=(b, N_EXPERTS),
            in_specs=[pl.BlockSpec((1, 1, 1, n), lambda bi, ei, st: (bi, ei, 0, 0)),
                      pl.BlockSpec((1, row0 + n, d), lambda bi, ei, st: (bi, 0, 0))],
            out_specs=pl.BlockSpec((1, 1, rows, d), lambda bi, ei, st: (bi, ei, 0, 0)),
            scratch_shapes=[pltpu.VMEM((rows, d), F32)]),
        out_shape=jax.ShapeDtypeStruct((b, N_EXPERTS, rows, d), BF16),
        compiler_params=_params(("parallel", "parallel")),
        name="moe_gather",
    )(starts, post4, u)


def _scatter_kernel(nt_r, starts_ref, out_ref, posn_ref, gate_ref, x_ref, pg_ref, g2_ref, xo_ref):
    bi, ti = pl.program_id(0), pl.program_id(1)
    rows = out_ref.shape[2]
    wide = min(RT + LANES, rows)
    narrow = min(LANES, rows)
    for sub in range(TM // RT):
        r0 = sub * RT
        y = jnp.zeros((RT, D_MODEL), F32)
        for e in range(N_EXPERTS):
            idx = (bi * N_EXPERTS + e) * (nt_r + 1) + ti * (TM // RT) + sub
            start = starts_ref[idx]
            count = starts_ref[idx + 1] - start

            def routed(width, align, e=e, r0=r0, start=start):
                a = _window(start, align, width, rows)
                j = lax.broadcasted_iota(jnp.int32, (RT, width), 1)
                p = posn_ref[0, r0:r0 + RT, e:e + 1]
                onehot = jnp.where(p - a == j, 1.0, 0.0).astype(BF16)
                return _dot(onehot, out_ref[0, e, pl.ds(a, width), :])

            if narrow == wide:
                got = routed(wide, LANES)
            else:
                fits = start - _window(start, BF16_ROWS, narrow, rows) + count <= narrow
                got = lax.cond(fits, functools.partial(routed, narrow, BF16_ROWS),
                               functools.partial(routed, wide, LANES))
            y = y + gate_ref[0, r0:r0 + RT, e:e + 1] * got
        y = y * lax.rsqrt(jnp.mean(y * y, axis=-1, keepdims=True) + EPS) * pg_ref[...]
        xo_ref[0, r0:r0 + RT, :] = x_ref[0, r0:r0 + RT, :] + g2_ref[0, 0] * y


def _scatter(starts, out, posn, gate, xc, pg, g2, seg, in_tile0, out_tile0, out_rows, alias):
    b, n, _ = posn.shape
    d = xc.shape[-1]
    rows = out.shape[2]
    nt_r = n // RT
    return pl.pallas_call(
        functools.partial(_scatter_kernel, nt_r),
        grid_spec=pltpu.PrefetchScalarGridSpec(
            num_scalar_prefetch=1,
            grid=(b, n // TM),
            in_specs=[pl.BlockSpec((1, N_EXPERTS, rows, d), lambda bi, ti, st: (bi, 0, 0, 0)),
                      pl.BlockSpec((1, TM, LANES), lambda bi, ti, st: (bi, ti, 0)),
                      pl.BlockSpec((1, TM, LANES), lambda bi, ti, st: (bi, ti, 0)),
                      pl.BlockSpec((1, TM, d), lambda bi, ti, st: (bi, ti + in_tile0, 0)),
                      pl.BlockSpec((1, d), lambda bi, ti, st: (0, 0)),
                      pl.BlockSpec((1, 1, 1, d), lambda bi, ti, st: (bi, seg, 0, 0))],
            out_specs=pl.BlockSpec((1, TM, d), lambda bi, ti, st: (bi, ti + out_tile0, 0))),
        out_shape=jax.ShapeDtypeStruct((b, out_rows, d), F32),
        input_output_aliases={4: 0} if alias else {},
        compiler_params=_params(("parallel", "parallel")),
        name="moe_scatter",
    )(starts, out, posn, gate, xc, pg, g2)


def _moe_segment(xc, u, logits, row0, n, layer, wg, wu, wd, pg, g2, seg, out_tile0, out_rows, alias):
    cap = max(1, CAPACITY_FACTOR * n // N_EXPERTS)
    rows = max(cap, MIN_ROWS)
    post, posn, gate, starts = _route(logits, row0, n, cap)
    starts = starts[:, :, :n // RT + 1].reshape(-1)
    sel = _gather(starts, post, u, row0, rows)
    out = _expert_ffn(sel, layer, wg, wu, wd)
    return _scatter(starts, out, posn, gate, xc, pg, g2, seg, row0 // TM, out_tile0, out_rows, alias)


def _pad_cols(w, n):
    return jnp.pad(w, ((0, 0), (0, n - w.shape[1])))


def kernel(x, c, ctx, c_ctx, ada_w, ada_b, mix_pre_g, w_in, conv_w, conv_b, dt_bias, a_log, d_skip,
           ssd_norm_g, cm_dw_w, cm_dw_b, cm_ln_g, cm_ln_b, w_out, mix_post_g, moe_pre_g, router_w,
           exp_w_gate, exp_w_up, exp_w_down, moe_post_g):
    b = x.shape[0]
    s_tot = CTX_LEN + x.shape[1]
    head, body, body_tile0 = ctx, x, 0
    row = lambda v: v.reshape(1, -1)
    sub8 = lambda v: jnp.broadcast_to(v, v.shape[:-2] + (SUBLANES, v.shape[-1]))

    li = lax.broadcasted_iota(jnp.int32, (CHUNK, CHUNK), 0)
    si = lax.broadcasted_iota(jnp.int32, (CHUNK, CHUNK), 1)
    tri = jnp.stack([(si <= li), (si >= li)]).astype(F32)

    cond = jnp.zeros((8, D_MODEL), F32).at[:b].set(c).at[b].set(c_ctx)
    cond = cond * _sigmoid(cond)

    for i in range(DEPTH):
        last = i == DEPTH - 1
        mods = _matmul(cond, ada_w[i]) + ada_b[i]
        mods = mods.reshape(8, 6, D_MODEL)
        both = lambda j: jnp.stack([jnp.broadcast_to(mods[b, j], (b, D_MODEL)), mods[:b, j]], axis=1)[:, :, None, :]
        sh1, sc1, g1, sh2, sc2, g2 = [both(j) for j in range(6)]

        wi = w_in[i].astype(BF16)
        wxbc = wi[:, :XBC_DIM]
        wdtf = _pad_cols(wi[:, XBC_DIM:XBC_DIM + N_HEADS], LANES)
        wdtb = _pad_cols(wi[:, XBC_DIM + N_HEADS:SSD_IN_COLS], LANES)
        wz = wi[:, SSD_IN_COLS:Z_END]
        wval = wi[:, Z_END:Z_END + D_CONF]
        wgate = wi[:, Z_END + D_CONF:]
        dtbf = _pad_cols(row(dt_bias[i, 0]), LANES)
        dtbb = _pad_cols(row(dt_bias[i, 1]), LANES)
        xbc, dt, zs, v = _mix_in(head, body, body_tile0, s_tot, mix_pre_g[i] * (1.0 + sc1), sh1, wxbc, wdtf, wdtb, wz, wval, wgate,
                                 dtbf, dtbb)

        xbcs = _conv_silu(xbc, conv_w[i], row(conv_b[i]))

        a_dir = _pad_cols(-jnp.exp(a_log[i]), LANES)[:, None, :]
        yf, yb = _ssd_scan(xbcs, dt, tri, a_dir, row(jnp.repeat(d_skip[i], HEAD_DIM)))

        wo = w_out[i].astype(BF16)
        rw = _pad_cols(router_w[i].astype(BF16), LANES)
        xc, u, logits = _mix_out(yf, yb, zs, v, head, body, body_tile0, row(ssd_norm_g[i]), sub8(cm_dw_w[i][:, None, :]), sub8(row(cm_dw_b[i])),
                                 row(cm_ln_g[i]), row(cm_ln_b[i]), wo[:D_SSD], wo[D_SSD:],
                                 row(mix_post_g[i]), g1, moe_pre_g[i] * (1.0 + sc2), sh2, rw)

        wg, wu, wd = exp_w_gate, exp_w_up, exp_w_down
        pg = row(moe_post_g[i])
        n_lat = s_tot - CTX_LEN
        if not last:
            xc = _moe_segment(xc, u, logits, CTX_LEN, n_lat, i, wg, wu, wd, pg, g2, 1, CTX_LEN // TM, s_tot, True)
            xc = _moe_segment(xc, u, logits, 0, CTX_LEN, i, wg, wu, wd, pg, g2, 0, 0, s_tot, True)
            head, body, body_tile0 = xc, xc, CTX_LEN // TM
        else:
            xc = _moe_segment(xc, u, logits, CTX_LEN, n_lat, i, wg, wu, wd, pg, g2, 1, 0, n_lat, False)
    return xc
```

```python
import functools

import jax
import jax.numpy as jnp
from jax import lax
from jax.experimental import pallas as pl
from jax.experimental.pallas import tpu as pltpu

D_MODEL = 1024
DEPTH = 2
CTX_LEN = 256
D_SSD = 1024
HEAD_DIM = 64
N_HEADS = D_SSD // HEAD_DIM
N_GROUPS = 4
HEADS_PER_GROUP = N_HEADS // N_GROUPS
D_STATE = 128
D_CONF = 1024
SSD_CONV = 5
CONF_KERNEL = 31
BC_DIM = N_GROUPS * D_STATE
XBC_DIM = D_SSD + 2 * BC_DIM
SSD_IN_COLS = XBC_DIM + 2 * N_HEADS
Z_END = SSD_IN_COLS + D_SSD
N_EXPERTS = 16
CAPACITY_FACTOR = 2
EPS = 1e-6

LANES = 128
SUBLANES = 8
BF16_ROWS = 16
TM = 256
CHUNK = 128
VMEM_LIMIT_BYTES = 56 * 1024 * 1024

F32 = jnp.float32
BF16 = jnp.bfloat16


def _dot(a, b):
    return jnp.dot(a, b, preferred_element_type=F32)


def _sigmoid(x):
    return 1.0 / (1.0 + jnp.exp(-x))


def _params(sem):
    return pltpu.CompilerParams(dimension_semantics=sem, vmem_limit_bytes=VMEM_LIMIT_BYTES)


def _mm_kernel(a_ref, w_ref, o_ref):
    o_ref[...] = _dot(a_ref[...].astype(BF16), w_ref[...].astype(BF16))


def _matmul(a, w, tn=512):
    m, k = a.shape
    n = w.shape[1]
    return pl.pallas_call(
        _mm_kernel,
        grid=(n // tn,),
        in_specs=[pl.BlockSpec((m, k), lambda j: (0, 0)),
                  pl.BlockSpec((k, tn), lambda j: (0, j))],
        out_specs=pl.BlockSpec((m, tn), lambda j: (0, j)),
        out_shape=jax.ShapeDtypeStruct((m, n), F32),
        compiler_params=_params(("parallel",)),
        name="adaln_matmul",
    )(a, w)


def _mix_in_kernel(head_ref, body_ref, scale_ref, shift_ref, wxbc_ref, wdtf_ref, wdtb_ref, wz_ref, wval_ref,
                   wgate_ref, dtbf_ref, dtbb_ref, xbc_ref, dt_ref, zs_ref, v_ref):
    x = _stream_tile(head_ref, body_ref)
    ms = jnp.mean(x * x, axis=-1, keepdims=True)
    u = x * lax.rsqrt(ms + EPS) * scale_ref[0, 0] + shift_ref[0, 0]
    ub = u.astype(BF16)
    xbc_ref[0] = _dot(ub, wxbc_ref[...]).astype(BF16)

    def softplus(t):
        return jnp.maximum(t, 0.0) + jnp.log(1.0 + jnp.exp(-jnp.abs(t)))

    dt_ref[0, 0] = softplus(_dot(ub, wdtf_ref[...]) + dtbf_ref[...])
    dt_ref[1, 0] = softplus(_dot(ub, wdtb_ref[...]) + dtbb_ref[...])
    z = _dot(ub, wz_ref[...])
    zs_ref[0] = (z * _sigmoid(z)).astype(BF16)
    val = _dot(ub, wval_ref[...])
    gate = _dot(ub, wgate_ref[...])
    v_ref[0] = (val * _sigmoid(gate)).astype(BF16)


def _seg(t):
    return jnp.minimum(t, 1)


def _stream_specs(body_tile0, d):
    head = pl.BlockSpec((1, TM, d), lambda bi, t: (bi, 0, 0))
    body = pl.BlockSpec((1, TM, d), lambda bi, t: (bi, jnp.maximum(t - 1 + body_tile0, 0), 0))
    return head, body


def _stream_tile(head_ref, body_ref):
    return jnp.where(pl.program_id(1) == 0, head_ref[0], body_ref[0])


def _mix_in(head, body, body_tile0, s, scale, shift, wxbc, wdtf, wdtb, wz, wval, wgate, dtbf, dtbb):
    b, _, d = head.shape
    head_spec, body_spec = _stream_specs(body_tile0, d)
    full = lambda shape: pl.BlockSpec(shape, lambda bi, t: (0,) * len(shape))
    mod = pl.BlockSpec((1, 1, 1, d), lambda bi, t: (bi, _seg(t), 0, 0))
    return pl.pallas_call(
        _mix_in_kernel,
        grid=(b, s // TM),
        in_specs=[head_spec, body_spec, mod, mod,
                  full(wxbc.shape), full(wdtf.shape), full(wdtb.shape), full(wz.shape),
                  full(wval.shape), full(wgate.shape), full(dtbf.shape), full(dtbb.shape)],
        out_specs=[pl.BlockSpec((1, TM, XBC_DIM), lambda bi, t: (bi, t, 0)),
                   pl.BlockSpec((2, 1, TM, LANES), lambda bi, t: (0, bi, t, 0)),
                   pl.BlockSpec((1, TM, D_SSD), lambda bi, t: (bi, t, 0)),
                   pl.BlockSpec((1, TM, D_CONF), lambda bi, t: (bi, t, 0))],
        out_shape=[jax.ShapeDtypeStruct((b, s, XBC_DIM), BF16),
                   jax.ShapeDtypeStruct((2, b, s, LANES), F32),
                   jax.ShapeDtypeStruct((b, s, D_SSD), BF16),
                   jax.ShapeDtypeStruct((b, s, D_CONF), BF16)],
        compiler_params=_params(("parallel", "parallel")),
        name="mix_in",
    )(head, body, scale, shift, wxbc, wdtf, wdtb, wz, wval, wgate, dtbf, dtbb)


def _halo_specs(width, n_tiles):
    per_tile = TM // BF16_ROWS
    n_blocks = n_tiles * per_tile
    prev = pl.BlockSpec((1, BF16_ROWS, width),
                        lambda bi, t: (bi, jnp.maximum(t * per_tile - 1, 0), 0))
    nxt = pl.BlockSpec((1, BF16_ROWS, width),
                       lambda bi, t: (bi, jnp.minimum((t + 1) * per_tile, n_blocks - 1), 0))
    return prev, nxt


def _fill_ext(ext_ref, prev_ref, cur_ref, next_ref, n_tiles):
    t = pl.program_id(1)
    prev_ok = jnp.where(t >= 2, 1.0, 0.0)
    next_ok = jnp.where(jnp.logical_and(t >= 1, t < n_tiles - 1), 1.0, 0.0)
    ext_ref[0:BF16_ROWS, :] = prev_ref[0].astype(F32) * prev_ok
    ext_ref[BF16_ROWS:BF16_ROWS + TM, :] = cur_ref[0].astype(F32)
    ext_ref[BF16_ROWS + TM:, :] = next_ref[0].astype(F32) * next_ok


CONV_ROWS = 128
CONV_COLS = 512


def _conv_silu_kernel(n_tiles, prev_ref, cur_ref, next_ref, w_ref, b_ref, o_ref, ext_ref):
    t = pl.program_id(1)
    prev_ok = jnp.where(t >= 2, 1.0, 0.0)
    next_ok = jnp.where(jnp.logical_and(t >= 1, t < n_tiles - 1), 1.0, 0.0)
    ext_ref[0:BF16_ROWS, :] = (prev_ref[0].astype(F32) * prev_ok).astype(BF16)
    ext_ref[BF16_ROWS:BF16_ROWS + TM, :] = cur_ref[0]
    ext_ref[BF16_ROWS + TM:, :] = (next_ref[0].astype(F32) * next_ok).astype(BF16)

    half = SSD_CONV // 2
    win_rows = CONV_ROWS + 2 * BF16_ROWS
    r = lax.broadcasted_iota(jnp.int32, ((SSD_CONV - 1) * CONV_ROWS, win_rows), 0)
    c = lax.broadcasted_iota(jnp.int32, ((SSD_CONV - 1) * CONV_ROWS, win_rows), 1)
    tap = jnp.right_shift(r, CONV_ROWS.bit_length() - 1)
    off = tap - half + jnp.where(tap >= half, 1, 0)
    shift = jnp.where(c == (r & (CONV_ROWS - 1)) + BF16_ROWS + off, 1.0, 0.0).astype(BF16)
    side_taps = [k for k in range(SSD_CONV) if k != half]

    for r0 in range(0, TM, CONV_ROWS):
        for c0 in range(0, XBC_DIM, CONV_COLS):
            cols = pl.ds(c0, CONV_COLS)
            shifted = _dot(shift, ext_ref[r0:r0 + win_rows, cols])
            centre = ext_ref[r0 + BF16_ROWS:r0 + BF16_ROWS + CONV_ROWS, cols].astype(F32)
            acc = b_ref[:, cols] + w_ref[half:half + 1, cols] * centre
            for i, k in enumerate(side_taps):
                acc = acc + w_ref[k:k + 1, cols] * shifted[i * CONV_ROWS:(i + 1) * CONV_ROWS, :]
            o_ref[0, r0:r0 + CONV_ROWS, cols] = (acc * _sigmoid(acc)).astype(BF16)


def _conv_silu(xbc, w, bias):
    b, s, c = xbc.shape
    n_tiles = s // TM
    prev, nxt = _halo_specs(c, n_tiles)
    return pl.pallas_call(
        functools.partial(_conv_silu_kernel, n_tiles),
        grid=(b, n_tiles),
        in_specs=[prev, pl.BlockSpec((1, TM, c), lambda bi, t: (bi, t, 0)), nxt,
                  pl.BlockSpec(w.shape, lambda bi, t: (0, 0)),
                  pl.BlockSpec(bias.shape, lambda bi, t: (0, 0))],
        out_specs=pl.BlockSpec((1, TM, c), lambda bi, t: (bi, t, 0)),
        out_shape=jax.ShapeDtypeStruct((b, s, c), BF16),
        scratch_shapes=[pltpu.VMEM((TM + 2 * BF16_ROWS, c), BF16)],
        compiler_params=_params(("parallel", "parallel")),
        name="conv_silu",
    )(xbc, xbc, xbc, w, bias)


def _split3(v):
    hi = v.astype(BF16)
    r1 = v - hi.astype(F32)
    mid = r1.astype(BF16)
    lo = (r1 - mid.astype(F32)).astype(BF16)
    return hi, mid, lo


def _ssd_kernel(xf_ref, xb_ref, dtf_ref, dtb_ref, tri_ref, a_ref, dskip_ref, yf_ref, yb_ref, state_ref):
    @pl.when(pl.program_id(1) == 0)
    def _():
        state_ref[...] = jnp.zeros_like(state_ref)

    _ssd_chunk(xf_ref, dtf_ref, tri_ref.at[0], a_ref.at[0], dskip_ref, yf_ref, state_ref.at[0])
    _ssd_chunk(xb_ref, dtb_ref, tri_ref.at[1], a_ref.at[1], None, yb_ref, state_ref.at[1])


def _ssd_chunk(x_ref, dt_ref, tri_ref, a_ref, dskip_ref, y_ref, state_ref):
    dt = dt_ref[0, 0]
    tri = tri_ref[...]
    da = dt * a_ref[...]
    tri_b = tri.astype(BF16)
    hi, mid, lo = _split3(da)
    cum = _dot(tri_b, hi) + _dot(tri_b, mid) + _dot(tri_b, lo)
    tot = jnp.sum(da, axis=0, keepdims=True)
    w = dt * jnp.exp(tot - cum)
    dec = jnp.exp(tot)
    cum_t = cum.T
    dt_t = dt.T
    w_t = w.T
    valid = tri > 0.0
    lane = lax.broadcasted_iota(jnp.int32, (CHUNK, 2 * HEAD_DIM), 1)
    first = lane < HEAD_DIM

    for g in range(N_GROUPS):
        b_g = x_ref[0, :, D_SSD + g * D_STATE:D_SSD + (g + 1) * D_STATE]
        c_g = x_ref[0, :, D_SSD + BC_DIM + g * D_STATE:D_SSD + BC_DIM + (g + 1) * D_STATE]
        b_gt = b_g.astype(F32).T
        scores = lax.dot_general(c_g, b_g, (((1,), (1,)), ((), ())), preferred_element_type=F32)
        gcols = slice(g * HEADS_PER_GROUP * HEAD_DIM, (g + 1) * HEADS_PER_GROUP * HEAD_DIM)
        y_off = _dot(c_g, state_ref[:, gcols].astype(BF16))
        for pair in range(HEADS_PER_GROUP // 2):
            h0 = g * HEADS_PER_GROUP + 2 * pair
            cols = slice(h0 * HEAD_DIM, (h0 + 2) * HEAD_DIM)
            x_p = x_ref[0, :, cols]
            y_heads, upd_heads, off_heads = [], [], []
            for h in (h0, h0 + 1):
                cum_l = jnp.broadcast_to(cum[:, h:h + 1], (CHUNK, CHUNK))
                seg = cum_l - cum_t[h:h + 1, :]
                decay = jnp.exp(jnp.where(valid, seg, -1e30))
                m = (scores * decay * dt_t[h:h + 1, :]).astype(BF16)
                y_heads.append(_dot(m, x_p))
                bw = (b_gt * w_t[h:h + 1, :]).astype(BF16)
                upd_heads.append(_dot(bw, x_p))
                off_heads.append(jnp.exp(cum_l))
            y_diag = jnp.where(first, y_heads[0], y_heads[1])
            upd = jnp.where(first, upd_heads[0], upd_heads[1])
            off_scale = jnp.where(first, off_heads[0], off_heads[1])
            pcols = slice(2 * pair * HEAD_DIM, (2 * pair + 2) * HEAD_DIM)
            y = y_diag + off_scale * y_off[:, pcols]
            if dskip_ref is not None:
                y = y + dskip_ref[:, cols] * x_p.astype(F32)
            y_ref[0, :, cols] = y.astype(BF16)
            dec_p = jnp.where(first[0:1, :], dec[:, h0:h0 + 1], dec[:, h0 + 1:h0 + 2])
            state_ref[:, cols] = dec_p * state_ref[:, cols] + upd


def _ssd_scan(xbcs, dt, tri, a_dir, dskip):
    b, s, _ = xbcs.shape
    nch = s // CHUNK
    nctx = CTX_LEN // CHUNK

    fwd = lambda k: k
    bwd = lambda k: jnp.where(k < nctx, nctx - 1 - k, nch + nctx - 1 - k)
    full = lambda a: pl.BlockSpec(a.shape, lambda bi, k: (0,) * a.ndim)
    x_spec = lambda c: pl.BlockSpec((1, CHUNK, XBC_DIM), lambda bi, k: (bi, c(k), 0))
    dt_spec = lambda d, c: pl.BlockSpec((1, 1, CHUNK, LANES), lambda bi, k: (d, bi, c(k), 0))
    y_spec = lambda c: pl.BlockSpec((1, CHUNK, D_SSD), lambda bi, k: (bi, c(k), 0))
    y_shape = jax.ShapeDtypeStruct((b, s, D_SSD), BF16)
    return pl.pallas_call(
        _ssd_kernel,
        grid=(b, nch),
        in_specs=[x_spec(fwd), x_spec(bwd), dt_spec(0, fwd), dt_spec(1, bwd),
                  full(tri), full(a_dir), full(dskip)],
        out_specs=[y_spec(fwd), y_spec(bwd)],
        out_shape=[y_shape, y_shape],
        scratch_shapes=[pltpu.VMEM((2, D_STATE, D_SSD), F32)],
        compiler_params=_params(("parallel", "arbitrary")),
        name="ssd_scan",
    )(xbcs, xbcs, dt, dt, tri, a_dir, dskip)


CM_ROWS = 32
CM_COLS = 512


def _mix_out_kernel(n_tiles, yf_ref, yb_ref, zs_ref, vprev_ref, v_ref, vnext_ref, head_ref, body_ref, ng_ref, cw_ref,
                    cb_ref, lg_ref, lb_ref, w1_ref, w2_ref, pg_ref, gate_ref, scale_ref, shift_ref,
                    rw_ref, xo_ref, u_ref, lo_ref, sh_ref, cv_ref):
    _fill_ext(sh_ref.at[0], vprev_ref, v_ref, vnext_ref, n_tiles)
    shifted_rows = TM + 2 * BF16_ROWS - SUBLANES
    for r in range(1, SUBLANES):
        sh_ref[r, 0:shifted_rows, :] = sh_ref[0, r:r + shifted_rows, :]
    half = CONF_KERNEL // 2

    def row_block(i, carry):
        base = pl.multiple_of(i * CM_ROWS, CM_ROWS)
        for c0 in range(0, D_CONF, CM_COLS):
            cols = pl.ds(c0, CM_COLS)
            accs = [cb_ref[:, cols] for _ in range(CM_ROWS // SUBLANES)]
            for k in range(CONF_KERNEL):
                q, r = divmod(BF16_ROWS - half + k, SUBLANES)
                w8 = cw_ref[k, :, cols]
                for g in range(CM_ROWS // SUBLANES):
                    accs[g] = accs[g] + w8 * sh_ref[r, pl.ds(base + (q + g) * SUBLANES, SUBLANES), cols]
            for g in range(CM_ROWS // SUBLANES):
                cv_ref[pl.ds(base + g * SUBLANES, SUBLANES), cols] = accs[g]
        return carry

    lax.fori_loop(0, TM // CM_ROWS, row_block, 0)

    cv = cv_ref[...]
    cc = cv - jnp.mean(cv, axis=-1, keepdims=True)
    ln = cc * lax.rsqrt(jnp.mean(cc * cc, axis=-1, keepdims=True) + EPS) * lg_ref[...] + lb_ref[...]
    vv = (ln * _sigmoid(ln)).astype(BF16)

    yz = (yf_ref[0].astype(F32) + yb_ref[0].astype(F32)) * zs_ref[0].astype(F32)
    y_ssd = yz * lax.rsqrt(jnp.mean(yz * yz, axis=-1, keepdims=True) + EPS) * ng_ref[...]
    m = _dot(y_ssd.astype(BF16), w1_ref[...]) + _dot(vv, w2_ref[...])
    m = m * lax.rsqrt(jnp.mean(m * m, axis=-1, keepdims=True) + EPS) * pg_ref[...]
    xn = _stream_tile(head_ref, body_ref) + gate_ref[0, 0] * m
    xo_ref[0] = xn
    u = xn * lax.rsqrt(jnp.mean(xn * xn, axis=-1, keepdims=True) + EPS) * scale_ref[0, 0] + shift_ref[0, 0]
    ub = u.astype(BF16)
    u_ref[0] = ub
    lo_ref[0] = _dot(ub, rw_ref[...])


def _mix_out(yf, yb, zs, v, head, body, body_tile0, ng, cw, cb, lg, lb, w1, w2, pg, gate, scale, shift, rw):
    b, s, _ = zs.shape
    d = head.shape[-1]
    n_tiles = s // TM
    head_spec, body_spec = _stream_specs(body_tile0, d)
    prev, nxt = _halo_specs(D_CONF, n_tiles)
    full = lambda a: pl.BlockSpec(a.shape, lambda bi, t: (0,) * a.ndim)
    mod = pl.BlockSpec((1, 1, 1, d), lambda bi, t: (bi, _seg(t), 0, 0))
    tile = lambda width: pl.BlockSpec((1, TM, width), lambda bi, t: (bi, t, 0))
    return pl.pallas_call(
        functools.partial(_mix_out_kernel, n_tiles),
        grid=(b, n_tiles),
        in_specs=[tile(D_SSD), tile(D_SSD), tile(D_SSD), prev, tile(D_CONF), nxt, head_spec, body_spec,
                  full(ng), full(cw), full(cb), full(lg), full(lb), full(w1), full(w2), full(pg),
                  mod, mod, mod, full(rw)],
        out_specs=[tile(d), tile(d), tile(LANES)],
        out_shape=[jax.ShapeDtypeStruct((b, s, d), F32),
                   jax.ShapeDtypeStruct((b, s, d), BF16),
                   jax.ShapeDtypeStruct((b, s, LANES), F32)],
        scratch_shapes=[pltpu.VMEM((SUBLANES, TM + 2 * BF16_ROWS, D_CONF), F32),
                        pltpu.VMEM((TM, D_CONF), F32)],
        compiler_params=_params(("parallel", "parallel")),
        name="mix_out",
    )(yf, yb, zs, v, v, v, head, body, ng, cw, cb, lg, lb, w1, w2, pg, gate, scale, shift, rw)


FFN_COLS = 256


FFN_MAX_ROWS = 1024


def _ffn_kernel(sel_ref, wg_ref, wu_ref, wd_ref, o_ref):
    nb, _, rows, d = sel_ref.shape
    s = sel_ref[:, 0].reshape(nb * rows, d)
    f = wg_ref.shape[-1]
    out = jnp.zeros((nb * rows, d), F32)
    for c0 in range(0, f, FFN_COLS):
        c1 = min(c0 + FFN_COLS, f)
        hg = _dot(s, wg_ref[0, 0, :, c0:c1].astype(BF16))
        hu = _dot(s, wu_ref[0, 0, :, c0:c1].astype(BF16))
        h = (hg * _sigmoid(hg) * hu).astype(BF16)
        out = out + _dot(h, wd_ref[0, 0, c0:c1, :].astype(BF16))
    o_ref[:, 0] = out.astype(BF16).reshape(nb, rows, d)


def _expert_ffn(sel, layer, wg, wu, wd):
    b, e, rows, d = sel.shape
    f = wg.shape[-1]
    nb = max(k for k in range(1, b + 1) if b % k == 0 and k * rows <= max(rows, FFN_MAX_ROWS))
    return pl.pallas_call(
        _ffn_kernel,
        grid=(e, b // nb),
        in_specs=[pl.BlockSpec((nb, 1, rows, d), lambda ei, bi: (bi, ei, 0, 0)),
                  pl.BlockSpec((1, 1, d, f), lambda ei, bi: (layer, ei, 0, 0)),
                  pl.BlockSpec((1, 1, d, f), lambda ei, bi: (layer, ei, 0, 0)),
                  pl.BlockSpec((1, 1, f, d), lambda ei, bi: (layer, ei, 0, 0))],
        out_specs=pl.BlockSpec((nb, 1, rows, d), lambda ei, bi: (bi, ei, 0, 0)),
        out_shape=jax.ShapeDtypeStruct((b, e, rows, d), BF16),
        compiler_params=_params(("parallel", "arbitrary")),
        name="expert_ffn",
    )(sel, wg, wu, wd)


RT = 128
BISECT_BITS = 31


def _route_kernel(cap, row0, lg_ref, post_ref, posn_ref, gate_ref, starts_ref, aff_ref, afft_ref):
    n = lg_ref.shape[1] - row0
    nt = n // RT
    lane_n = lax.broadcasted_iota(jnp.int32, (RT, LANES), 1)
    real_n = lane_n < N_EXPERTS

    for i in range(nt):
        rows = slice(i * RT, (i + 1) * RT)
        lg = jnp.where(real_n, lg_ref[0, row0 + i * RT:row0 + (i + 1) * RT, :], -1e30)
        ex = jnp.exp(lg - jnp.max(lg, axis=-1, keepdims=True))
        aff = ex / jnp.sum(ex, axis=-1, keepdims=True)
        aff_ref[rows, :] = aff
        afft_ref[:, rows] = aff.T[0:N_EXPERTS, :]

    bits = lax.bitcast_convert_type(afft_ref[...], jnp.int32)

    def bisect(i, prefix):
        cand = prefix | jnp.left_shift(jnp.int32(1), BISECT_BITS - 1 - i)
        cnt = jnp.sum(jnp.where(bits >= cand, 1.0, 0.0), axis=1, keepdims=True)
        return jnp.where(cnt >= cap, cand, prefix)

    thr = lax.fori_loop(0, BISECT_BITS, bisect, jnp.zeros((N_EXPERTS, 1), jnp.int32))
    n_gt = jnp.sum(jnp.where(bits > thr, 1.0, 0.0), axis=1, keepdims=True)
    need = cap - n_gt

    r_i = lax.broadcasted_iota(jnp.int32, (RT, RT), 0)
    c_i = lax.broadcasted_iota(jnp.int32, (RT, RT), 1)
    upper = jnp.where(r_i <= c_i, 1.0, 0.0).astype(BF16)
    lower = jnp.where(c_i <= r_i, 1.0, 0.0).astype(BF16)
    lane_s = lax.broadcasted_iota(jnp.int32, (N_EXPERTS, LANES), 1)
    carry_eq = jnp.zeros((N_EXPERTS, 1), F32)
    carry_sel = jnp.zeros((N_EXPERTS, 1), F32)
    starts = jnp.zeros((N_EXPERTS, LANES), F32)
    for i in range(nt):
        cols = slice(i * RT, (i + 1) * RT)
        b_t = bits[:, cols]
        eq = jnp.where(b_t == thr, 1.0, 0.0)
        rank = _dot(eq.astype(BF16), upper) + carry_eq
        sel = jnp.where(jnp.logical_or(b_t > thr, jnp.logical_and(b_t == thr, rank <= need)), 1.0, 0.0)
        pos = _dot(sel.astype(BF16), upper) + carry_sel - sel
        post_ref[0, :, cols] = jnp.where(sel > 0.0, pos, -1.0).astype(jnp.int32)
        starts = jnp.where(lane_s == i, carry_sel, starts)
        carry_eq = carry_eq + jnp.sum(eq, axis=1, keepdims=True)
        carry_sel = carry_sel + jnp.sum(sel, axis=1, keepdims=True)
    starts_ref[0] = starts.astype(jnp.int32)

    sub_s = lax.broadcasted_iota(jnp.int32, (N_EXPERTS, LANES), 0)
    eye = sub_s == lane_s
    thr_row = jnp.sum(jnp.where(eye, lax.bitcast_convert_type(thr, F32), 0.0), axis=0, keepdims=True)
    need_row = jnp.sum(jnp.where(eye, need, 0.0), axis=0, keepdims=True)
    carry_eq = jnp.zeros((1, LANES), F32)
    carry_sel = jnp.zeros((1, LANES), F32)
    for i in range(nt):
        rows = slice(i * RT, (i + 1) * RT)
        aff = aff_ref[rows, :]
        eq_b = jnp.logical_and(aff == thr_row, real_n)
        eq = jnp.where(eq_b, 1.0, 0.0)
        rank = _dot(lower, eq.astype(BF16)) + carry_eq
        sel_b = jnp.logical_and(real_n, jnp.logical_or(aff > thr_row, jnp.logical_and(eq_b, rank <= need_row)))
        sel = jnp.where(sel_b, 1.0, 0.0)
        pos = _dot(lower, sel.astype(BF16)) + carry_sel - sel
        posn_ref[0, rows, :] = jnp.where(sel_b, pos, -1.0).astype(jnp.int32)
        gate_ref[0, rows, :] = jnp.where(sel_b, aff, 0.0)
        carry_eq = carry_eq + jnp.sum(eq, axis=0, keepdims=True)
        carry_sel = carry_sel + jnp.sum(sel, axis=0, keepdims=True)


def _route(logits, row0, n, cap):
    b = logits.shape[0]
    return pl.pallas_call(
        functools.partial(_route_kernel, cap, row0),
        grid=(b,),
        in_specs=[pl.BlockSpec((1, row0 + n, LANES), lambda bi: (bi, 0, 0))],
        out_specs=[pl.BlockSpec((1, N_EXPERTS, n), lambda bi: (bi, 0, 0)),
                   pl.BlockSpec((1, n, LANES), lambda bi: (bi, 0, 0)),
                   pl.BlockSpec((1, n, LANES), lambda bi: (bi, 0, 0)),
                   pl.BlockSpec((1, N_EXPERTS, LANES), lambda bi: (bi, 0, 0))],
        out_shape=[jax.ShapeDtypeStruct((b, N_EXPERTS, n), jnp.int32),
                   jax.ShapeDtypeStruct((b, n, LANES), jnp.int32),
                   jax.ShapeDtypeStruct((b, n, LANES), F32),
                   jax.ShapeDtypeStruct((b, N_EXPERTS, LANES), jnp.int32)],
        scratch_shapes=[pltpu.VMEM((n, LANES), F32), pltpu.VMEM((N_EXPERTS, n), F32)],
        compiler_params=_params(("parallel",)),
        name="route",
    )(logits)


GT = 256
MIN_ROWS = 128


def _window(start, align, width, rows):
    if width == rows:
        return 0
    a = jnp.minimum((start // align) * align, rows - width)
    return pl.multiple_of(a, align)


def _gather_kernel(nt_r, row0, starts_ref, post_ref, u_ref, sel_ref, acc_ref):
    bi, ei = pl.program_id(0), pl.program_id(1)
    rows = acc_ref.shape[0]
    n = u_ref.shape[1] - row0
    gt = min(GT, n)
    width = min(gt + BF16_ROWS, rows)
    acc_ref[...] = jnp.zeros_like(acc_ref)
    j = lax.broadcasted_iota(jnp.int32, (width, gt), 0)
    for i in range(n // gt):
        start = starts_ref[(bi * N_EXPERTS + ei) * nt_r + i * (gt // RT)]
        a = _window(start, BF16_ROWS, width, rows)
        p = post_ref[0, 0, :, i * gt:(i + 1) * gt]
        onehot = jnp.where(p - a == j, 1.0, 0.0).astype(BF16)
        acc_ref[pl.ds(a, width), :] += _dot(onehot, u_ref[0, row0 + i * gt:row0 + (i + 1) * gt, :])
    sel_ref[0, 0] = acc_ref[...].astype(BF16)


def _gather(starts, post, u, row0, rows):
    b, _, d = u.shape
    n = post.shape[-1]
    nt_r = n // RT
    post4 = post.reshape(b, N_EXPERTS, 1, n)
    return pl.pallas_call(
        functools.partial(_gather_kernel, nt_r, row0),
        grid_spec=pltpu.PrefetchScalarGridSpec(
            num_scalar_prefetch=1,
            grid=(b, N_EXPERTS),
            in_specs=[pl.BlockSpec((1, 1, 1, n), lambda bi, ei, st: (bi, ei, 0, 0)),
                      pl.BlockSpec((1, row0 + n, d), lambda bi, ei, st: (bi, 0, 0))],
            out_specs=pl.BlockSpec((1, 1, rows, d), lambda bi, ei, st: (bi, ei, 0, 0)),
            scratch_shapes=[pltpu.VMEM((rows, d), F32)]),
        out_shape=jax.ShapeDtypeStruct((b, N_EXPERTS, rows, d), BF16),
        compiler_params=_params(("parallel", "parallel")),
        name="moe_gather",
    )(starts, post4, u)


def _scatter_kernel(nt_r, starts_ref, out_ref, posn_ref, gate_ref, x_ref, pg_ref, g2_ref, xo_ref):
    bi, ti = pl.program_id(0), pl.program_id(1)
    rows = out_ref.shape[2]
    width = min(RT + LANES, rows)
    j = lax.broadcasted_iota(jnp.int32, (RT, width), 1)
    for sub in range(TM // RT):
        r0 = sub * RT
        y = jnp.zeros((RT, D_MODEL), F32)
        for e in range(N_EXPERTS):
            start = starts_ref[(bi * N_EXPERTS + e) * nt_r + ti * (TM // RT) + sub]
            a = _window(start, LANES, width, rows)
            p = posn_ref[0, r0:r0 + RT, e:e + 1]
            onehot = jnp.where(p - a == j, 1.0, 0.0).astype(BF16)
            y = y + gate_ref[0, r0:r0 + RT, e:e + 1] * _dot(onehot, out_ref[0, e, pl.ds(a, width), :])
        y = y * lax.rsqrt(jnp.mean(y * y, axis=-1, keepdims=True) + EPS) * pg_ref[...]
        xo_ref[0, r0:r0 + RT, :] = x_ref[0, r0:r0 + RT, :] + g2_ref[0, 0] * y


def _scatter(starts, out, posn, gate, xc, pg, g2, seg, in_tile0, out_tile0, out_rows, alias):
    b, n, _ = posn.shape
    d = xc.shape[-1]
    rows = out.shape[2]
    nt_r = n // RT
    return pl.pallas_call(
        functools.partial(_scatter_kernel, nt_r),
        grid_spec=pltpu.PrefetchScalarGridSpec(
            num_scalar_prefetch=1,
            grid=(b, n // TM),
            in_specs=[pl.BlockSpec((1, N_EXPERTS, rows, d), lambda bi, ti, st: (bi, 0, 0, 0)),
                      pl.BlockSpec((1, TM, LANES), lambda bi, ti, st: (bi, ti, 0)),
                      pl.BlockSpec((1, TM, LANES), lambda bi, ti, st: (bi, ti, 0)),
                      pl.BlockSpec((1, TM, d), lambda bi, ti, st: (bi, ti + in_tile0, 0)),
                      pl.BlockSpec((1, d), lambda bi, ti, st: (0, 0)),
                      pl.BlockSpec((1, 1, 1, d), lambda bi, ti, st: (bi, seg, 0, 0))],
            out_specs=pl.BlockSpec((1, TM, d), lambda bi, ti, st: (bi, ti + out_tile0, 0))),
        out_shape=jax.ShapeDtypeStruct((b, out_rows, d), F32),
        input_output_aliases={4: 0} if alias else {},
        compiler_params=_params(("parallel", "parallel")),
        name="moe_scatter",
    )(starts, out, posn, gate, xc, pg, g2)


def _moe_segment(xc, u, logits, row0, n, layer, wg, wu, wd, pg, g2, seg, out_tile0, out_rows, alias):
    cap = max(1, CAPACITY_FACTOR * n // N_EXPERTS)
    rows = max(cap, MIN_ROWS)
    post, posn, gate, starts = _route(logits, row0, n, cap)
    starts = starts[:, :, :n // RT].reshape(-1)
    sel = _gather(starts, post, u, row0, rows)
    out = _expert_ffn(sel, layer, wg, wu, wd)
    return _scatter(starts, out, posn, gate, xc, pg, g2, seg, row0 // TM, out_tile0, out_rows, alias)


def _pad_cols(w, n):
    return jnp.pad(w, ((0, 0), (0, n - w.shape[1])))


def kernel(x, c, ctx, c_ctx, ada_w, ada_b, mix_pre_g, w_in, conv_w, conv_b, dt_bias, a_log, d_skip,
           ssd_norm_g, cm_dw_w, cm_dw_b, cm_ln_g, cm_ln_b, w_out, mix_post_g, moe_pre_g, router_w,
           exp_w_gate, exp_w_up, exp_w_down, moe_post_g):
    b = x.shape[0]
    s_tot = CTX_LEN + x.shape[1]
    head, body, body_tile0 = ctx, x, 0
    row = lambda v: v.reshape(1, -1)
    sub8 = lambda v: jnp.broadcast_to(v, v.shape[:-2] + (SUBLANES, v.shape[-1]))

    li = lax.broadcasted_iota(jnp.int32, (CHUNK, CHUNK), 0)
    si = lax.broadcasted_iota(jnp.int32, (CHUNK, CHUNK), 1)
    tri = jnp.stack([(si <= li), (si >= li)]).astype(F32)

    cond = jnp.zeros((8, D_MODEL), F32).at[:b].set(c).at[b].set(c_ctx)
    cond = cond * _sigmoid(cond)

    for i in range(DEPTH):
        last = i == DEPTH - 1
        mods = _matmul(cond, ada_w[i]) + ada_b[i]
        mods = mods.reshape(8, 6, D_MODEL)
        both = lambda j: jnp.stack([jnp.broadcast_to(mods[b, j], (b, D_MODEL)), mods[:b, j]], axis=1)[:, :, None, :]
        sh1, sc1, g1, sh2, sc2, g2 = [both(j) for j in range(6)]

        wi = w_in[i].astype(BF16)
        wxbc = wi[:, :XBC_DIM]
        wdtf = _pad_cols(wi[:, XBC_DIM:XBC_DIM + N_HEADS], LANES)
        wdtb = _pad_cols(wi[:, XBC_DIM + N_HEADS:SSD_IN_COLS], LANES)
        wz = wi[:, SSD_IN_COLS:Z_END]
        wval = wi[:, Z_END:Z_END + D_CONF]
        wgate = wi[:, Z_END + D_CONF:]
        dtbf = _pad_cols(row(dt_bias[i, 0]), LANES)
        dtbb = _pad_cols(row(dt_bias[i, 1]), LANES)
        xbc, dt, zs, v = _mix_in(head, body, body_tile0, s_tot, mix_pre_g[i] * (1.0 + sc1), sh1, wxbc, wdtf, wdtb, wz, wval, wgate,
                                 dtbf, dtbb)

        xbcs = _conv_silu(xbc, conv_w[i], row(conv_b[i]))

        a_dir = _pad_cols(-jnp.exp(a_log[i]), LANES)[:, None, :]
        yf, yb = _ssd_scan(xbcs, dt, tri, a_dir, row(jnp.repeat(d_skip[i], HEAD_DIM)))

        wo = w_out[i].astype(BF16)
        rw = _pad_cols(router_w[i].astype(BF16), LANES)
        xc, u, logits = _mix_out(yf, yb, zs, v, head, body, body_tile0, row(ssd_norm_g[i]), sub8(cm_dw_w[i][:, None, :]), sub8(row(cm_dw_b[i])),
                                 row(cm_ln_g[i]), row(cm_ln_b[i]), wo[:D_SSD], wo[D_SSD:],
                                 row(mix_post_g[i]), g1, moe_pre_g[i] * (1.0 + sc2), sh2, rw)

        wg, wu, wd = exp_w_gate, exp_w_up, exp_w_down
        pg = row(moe_post_g[i])
        n_lat = s_tot - CTX_LEN
        if not last:
            xc = _moe_segment(xc, u, logits, CTX_LEN, n_lat, i, wg, wu, wd, pg, g2, 1, CTX_LEN // TM, s_tot, True)
            xc = _moe_segment(xc, u, logits, 0, CTX_LEN, i, wg, wu, wd, pg, g2, 0, 0, s_tot, True)
            head, body, body_tile0 = xc, xc, CTX_LEN // TM
        else:
            xc = _moe_segment(xc, u, logits, CTX_LEN, n_lat, i, wg, wu, wd, pg, g2, 1, 0, n_lat, False)
    return xc
```

```python
import functools

import jax
import jax.numpy as jnp
from jax import lax
from jax.experimental import pallas as pl
from jax.experimental.pallas import tpu as pltpu

D_MODEL = 1024
DEPTH = 2
CTX_LEN = 256
D_SSD = 1024
HEAD_DIM = 64
N_HEADS = D_SSD // HEAD_DIM
N_GROUPS = 4
HEADS_PER_GROUP = N_HEADS // N_GROUPS
D_STATE = 128
D_CONF = 1024
SSD_CONV = 5
CONF_KERNEL = 31
BC_DIM = N_GROUPS * D_STATE
XBC_DIM = D_SSD + 2 * BC_DIM
SSD_IN_COLS = XBC_DIM + 2 * N_HEADS
Z_END = SSD_IN_COLS + D_SSD
N_EXPERTS = 16
CAPACITY_FACTOR = 2
EPS = 1e-6

LANES = 128
SUBLANES = 8
BF16_ROWS = 16
TM = 256
CHUNK = 128
SSD_STEP = 256
VMEM_LIMIT_BYTES = 56 * 1024 * 1024

F32 = jnp.float32
BF16 = jnp.bfloat16


def _dot(a, b):
    return jnp.dot(a, b, preferred_element_type=F32)


def _sigmoid(x):
    return 1.0 / (1.0 + jnp.exp(-x))


def _params(sem):
    return pltpu.CompilerParams(dimension_semantics=sem, vmem_limit_bytes=VMEM_LIMIT_BYTES)


def _mm_kernel(a_ref, w_ref, o_ref):
    o_ref[...] = _dot(a_ref[...].astype(BF16), w_ref[...].astype(BF16))


def _matmul(a, w, tn=512):
    m, k = a.shape
    n = w.shape[1]
    return pl.pallas_call(
        _mm_kernel,
        grid=(n // tn,),
        in_specs=[pl.BlockSpec((m, k), lambda j: (0, 0)),
                  pl.BlockSpec((k, tn), lambda j: (0, j))],
        out_specs=pl.BlockSpec((m, tn), lambda j: (0, j)),
        out_shape=jax.ShapeDtypeStruct((m, n), F32),
        compiler_params=_params(("parallel",)),
        name="adaln_matmul",
    )(a, w)


def _mix_in_kernel(head_ref, body_ref, scale_ref, shift_ref, wxbc_ref, wdtf_ref, wdtb_ref, wz_ref, wval_ref,
                   wgate_ref, dtbf_ref, dtbb_ref, xbc_ref, dt_ref, zs_ref, v_ref):
    x = _stream_tile(head_ref, body_ref)
    ms = jnp.mean(x * x, axis=-1, keepdims=True)
    u = x * lax.rsqrt(ms + EPS) * scale_ref[0, 0] + shift_ref[0, 0]
    ub = u.astype(BF16)
    xbc_ref[0] = _dot(ub, wxbc_ref[...]).astype(BF16)

    def softplus(t):
        return jnp.maximum(t, 0.0) + jnp.log(1.0 + jnp.exp(-jnp.abs(t)))

    dt_ref[0, 0] = softplus(_dot(ub, wdtf_ref[...]) + dtbf_ref[...])
    dt_ref[1, 0] = softplus(_dot(ub, wdtb_ref[...]) + dtbb_ref[...])
    z = _dot(ub, wz_ref[...])
    zs_ref[0] = (z * _sigmoid(z)).astype(BF16)
    val = _dot(ub, wval_ref[...])
    gate = _dot(ub, wgate_ref[...])
    v_ref[0] = (val * _sigmoid(gate)).astype(BF16)


def _seg(t):
    return jnp.minimum(t, 1)


def _stream_specs(body_tile0, d):
    head = pl.BlockSpec((1, TM, d), lambda bi, t: (bi, 0, 0))
    body = pl.BlockSpec((1, TM, d), lambda bi, t: (bi, jnp.maximum(t - 1 + body_tile0, 0), 0))
    return head, body


def _stream_tile(head_ref, body_ref):
    return jnp.where(pl.program_id(1) == 0, head_ref[0], body_ref[0])


def _mix_in(head, body, body_tile0, s, scale, shift, wxbc, wdtf, wdtb, wz, wval, wgate, dtbf, dtbb):
    b, _, d = head.shape
    head_spec, body_spec = _stream_specs(body_tile0, d)
    full = lambda shape: pl.BlockSpec(shape, lambda bi, t: (0,) * len(shape))
    mod = pl.BlockSpec((1, 1, 1, d), lambda bi, t: (bi, _seg(t), 0, 0))
    return pl.pallas_call(
        _mix_in_kernel,
        grid=(b, s // TM),
        in_specs=[head_spec, body_spec, mod, mod,
                  full(wxbc.shape), full(wdtf.shape), full(wdtb.shape), full(wz.shape),
                  full(wval.shape), full(wgate.shape), full(dtbf.shape), full(dtbb.shape)],
        out_specs=[pl.BlockSpec((1, TM, XBC_DIM), lambda bi, t: (bi, t, 0)),
                   pl.BlockSpec((2, 1, TM, LANES), lambda bi, t: (0, bi, t, 0)),
                   pl.BlockSpec((1, TM, D_SSD), lambda bi, t: (bi, t, 0)),
                   pl.BlockSpec((1, TM, D_CONF), lambda bi, t: (bi, t, 0))],
        out_shape=[jax.ShapeDtypeStruct((b, s, XBC_DIM), BF16),
                   jax.ShapeDtypeStruct((2, b, s, LANES), F32),
                   jax.ShapeDtypeStruct((b, s, D_SSD), BF16),
                   jax.ShapeDtypeStruct((b, s, D_CONF), BF16)],
        compiler_params=_params(("parallel", "parallel")),
        name="mix_in",
    )(head, body, scale, shift, wxbc, wdtf, wdtb, wz, wval, wgate, dtbf, dtbb)


def _halo_specs(width, n_tiles):
    per_tile = TM // BF16_ROWS
    n_blocks = n_tiles * per_tile
    prev = pl.BlockSpec((1, BF16_ROWS, width),
                        lambda bi, t: (bi, jnp.maximum(t * per_tile - 1, 0), 0))
    nxt = pl.BlockSpec((1, BF16_ROWS, width),
                       lambda bi, t: (bi, jnp.minimum((t + 1) * per_tile, n_blocks - 1), 0))
    return prev, nxt


def _fill_ext(ext_ref, prev_ref, cur_ref, next_ref, n_tiles):
    t = pl.program_id(1)
    prev_ok = jnp.where(t >= 2, 1.0, 0.0)
    next_ok = jnp.where(jnp.logical_and(t >= 1, t < n_tiles - 1), 1.0, 0.0)
    ext_ref[0:BF16_ROWS, :] = prev_ref[0].astype(F32) * prev_ok
    ext_ref[BF16_ROWS:BF16_ROWS + TM, :] = cur_ref[0].astype(F32)
    ext_ref[BF16_ROWS + TM:, :] = next_ref[0].astype(F32) * next_ok


CONV_ROWS = 128
CONV_COLS = 512


def _conv_silu_kernel(n_tiles, prev_ref, cur_ref, next_ref, w_ref, b_ref, o_ref, ext_ref):
    t = pl.program_id(1)
    prev_ok = jnp.where(t >= 2, 1.0, 0.0)
    next_ok = jnp.where(jnp.logical_and(t >= 1, t < n_tiles - 1), 1.0, 0.0)
    ext_ref[0:BF16_ROWS, :] = (prev_ref[0].astype(F32) * prev_ok).astype(BF16)
    ext_ref[BF16_ROWS:BF16_ROWS + TM, :] = cur_ref[0]
    ext_ref[BF16_ROWS + TM:, :] = (next_ref[0].astype(F32) * next_ok).astype(BF16)

    half = SSD_CONV // 2
    win_rows = CONV_ROWS + 2 * BF16_ROWS
    r = lax.broadcasted_iota(jnp.int32, ((SSD_CONV - 1) * CONV_ROWS, win_rows), 0)
    c = lax.broadcasted_iota(jnp.int32, ((SSD_CONV - 1) * CONV_ROWS, win_rows), 1)
    tap = jnp.right_shift(r, CONV_ROWS.bit_length() - 1)
    off = tap - half + jnp.where(tap >= half, 1, 0)
    shift = jnp.where(c == (r & (CONV_ROWS - 1)) + BF16_ROWS + off, 1.0, 0.0).astype(BF16)
    side_taps = [k for k in range(SSD_CONV) if k != half]

    for r0 in range(0, TM, CONV_ROWS):
        for c0 in range(0, XBC_DIM, CONV_COLS):
            cols = pl.ds(c0, CONV_COLS)
            shifted = _dot(shift, ext_ref[r0:r0 + win_rows, cols])
            centre = ext_ref[r0 + BF16_ROWS:r0 + BF16_ROWS + CONV_ROWS, cols].astype(F32)
            acc = b_ref[:, cols] + w_ref[half:half + 1, cols] * centre
            for i, k in enumerate(side_taps):
                acc = acc + w_ref[k:k + 1, cols] * shifted[i * CONV_ROWS:(i + 1) * CONV_ROWS, :]
            o_ref[0, r0:r0 + CONV_ROWS, cols] = (acc * _sigmoid(acc)).astype(BF16)


def _conv_silu(xbc, w, bias):
    b, s, c = xbc.shape
    n_tiles = s // TM
    prev, nxt = _halo_specs(c, n_tiles)
    return pl.pallas_call(
        functools.partial(_conv_silu_kernel, n_tiles),
        grid=(b, n_tiles),
        in_specs=[prev, pl.BlockSpec((1, TM, c), lambda bi, t: (bi, t, 0)), nxt,
                  pl.BlockSpec(w.shape, lambda bi, t: (0, 0)),
                  pl.BlockSpec(bias.shape, lambda bi, t: (0, 0))],
        out_specs=pl.BlockSpec((1, TM, c), lambda bi, t: (bi, t, 0)),
        out_shape=jax.ShapeDtypeStruct((b, s, c), BF16),
        scratch_shapes=[pltpu.VMEM((TM + 2 * BF16_ROWS, c), BF16)],
        compiler_params=_params(("parallel", "parallel")),
        name="conv_silu",
    )(xbc, xbc, xbc, w, bias)


def _split3(v):
    hi = v.astype(BF16)
    r1 = v - hi.astype(F32)
    mid = r1.astype(BF16)
    lo = (r1 - mid.astype(F32)).astype(BF16)
    return hi, mid, lo


def _ssd_kernel(xf_ref, xb_ref, dtf_ref, dtb_ref, tri_ref, a_ref, dskip_ref, yf_ref, yb_ref, state_ref):
    @pl.when(pl.program_id(1) == 0)
    def _():
        state_ref[...] = jnp.zeros_like(state_ref)

    for r0 in range(0, SSD_STEP, CHUNK):
        _ssd_chunk(r0, xf_ref, dtf_ref, tri_ref.at[0], a_ref.at[0], dskip_ref, yf_ref, state_ref.at[0])
        _ssd_chunk(SSD_STEP - CHUNK - r0, xb_ref, dtb_ref, tri_ref.at[1], a_ref.at[1], None, yb_ref,
                   state_ref.at[1])


def _ssd_chunk(r0, x_ref, dt_ref, tri_ref, a_ref, dskip_ref, y_ref, state_ref):
    rows = slice(r0, r0 + CHUNK)
    dt = dt_ref[0, 0, rows, :]
    tri = tri_ref[...]
    da = dt * a_ref[...]
    tri_b = tri.astype(BF16)
    hi, mid, lo = _split3(da)
    cum = _dot(tri_b, hi) + _dot(tri_b, mid) + _dot(tri_b, lo)
    tot = jnp.sum(da, axis=0, keepdims=True)
    w = dt * jnp.exp(tot - cum)
    dec = jnp.exp(tot)
    cum_t = cum.T
    dt_t = dt.T
    w_t = w.T
    valid = tri > 0.0
    lane = lax.broadcasted_iota(jnp.int32, (CHUNK, 2 * HEAD_DIM), 1)
    first = lane < HEAD_DIM

    for g in range(N_GROUPS):
        b_g = x_ref[0, rows, D_SSD + g * D_STATE:D_SSD + (g + 1) * D_STATE]
        c_g = x_ref[0, rows, D_SSD + BC_DIM + g * D_STATE:D_SSD + BC_DIM + (g + 1) * D_STATE]
        b_gt = b_g.astype(F32).T
        scores = lax.dot_general(c_g, b_g, (((1,), (1,)), ((), ())), preferred_element_type=F32)
        gcols = slice(g * HEADS_PER_GROUP * HEAD_DIM, (g + 1) * HEADS_PER_GROUP * HEAD_DIM)
        y_off = _dot(c_g, state_ref[:, gcols].astype(BF16))
        for pair in range(HEADS_PER_GROUP // 2):
            h0 = g * HEADS_PER_GROUP + 2 * pair
            cols = slice(h0 * HEAD_DIM, (h0 + 2) * HEAD_DIM)
            x_p = x_ref[0, rows, cols]
            y_heads, upd_heads, off_heads = [], [], []
            for h in (h0, h0 + 1):
                cum_l = jnp.broadcast_to(cum[:, h:h + 1], (CHUNK, CHUNK))
                seg = cum_l - cum_t[h:h + 1, :]
                decay = jnp.exp(jnp.where(valid, seg, -1e30))
                m = (scores * decay * dt_t[h:h + 1, :]).astype(BF16)
                y_heads.append(_dot(m, x_p))
                bw = (b_gt * w_t[h:h + 1, :]).astype(BF16)
                upd_heads.append(_dot(bw, x_p))
                off_heads.append(jnp.exp(cum_l))
            y_diag = jnp.where(first, y_heads[0], y_heads[1])
            upd = jnp.where(first, upd_heads[0], upd_heads[1])
            off_scale = jnp.where(first, off_heads[0], off_heads[1])
            pcols = slice(2 * pair * HEAD_DIM, (2 * pair + 2) * HEAD_DIM)
            y = y_diag + off_scale * y_off[:, pcols]
            if dskip_ref is not None:
                y = y + dskip_ref[:, cols] * x_p.astype(F32)
            y_ref[0, rows, cols] = y.astype(BF16)
            dec_p = jnp.where(first[0:1, :], dec[:, h0:h0 + 1], dec[:, h0 + 1:h0 + 2])
            state_ref[:, cols] = dec_p * state_ref[:, cols] + upd


def _ssd_scan(xbcs, dt, tri, a_dir, dskip):
    b, s, _ = xbcs.shape
    nch = s // SSD_STEP
    nctx = CTX_LEN // SSD_STEP

    fwd = lambda k: k
    bwd = lambda k: jnp.where(k < nctx, nctx - 1 - k, nch + nctx - 1 - k)
    full = lambda a: pl.BlockSpec(a.shape, lambda bi, k: (0,) * a.ndim)
    x_spec = lambda c: pl.BlockSpec((1, SSD_STEP, XBC_DIM), lambda bi, k: (bi, c(k), 0))
    dt_spec = lambda d, c: pl.BlockSpec((1, 1, SSD_STEP, LANES), lambda bi, k: (d, bi, c(k), 0))
    y_spec = lambda c: pl.BlockSpec((1, SSD_STEP, D_SSD), lambda bi, k: (bi, c(k), 0))
    y_shape = jax.ShapeDtypeStruct((b, s, D_SSD), BF16)
    return pl.pallas_call(
        _ssd_kernel,
        grid=(b, nch),
        in_specs=[x_spec(fwd), x_spec(bwd), dt_spec(0, fwd), dt_spec(1, bwd),
                  full(tri), full(a_dir), full(dskip)],
        out_specs=[y_spec(fwd), y_spec(bwd)],
        out_shape=[y_shape, y_shape],
        scratch_shapes=[pltpu.VMEM((2, D_STATE, D_SSD), F32)],
        compiler_params=_params(("parallel", "arbitrary")),
        name="ssd_scan",
    )(xbcs, xbcs, dt, dt, tri, a_dir, dskip)


CM_ROWS = 32
CM_COLS = 512


def _mix_out_kernel(n_tiles, yf_ref, yb_ref, zs_ref, vprev_ref, v_ref, vnext_ref, head_ref, body_ref, ng_ref, cw_ref,
                    cb_ref, lg_ref, lb_ref, w1_ref, w2_ref, pg_ref, gate_ref, scale_ref, shift_ref,
                    rw_ref, xo_ref, u_ref, lo_ref, sh_ref, cv_ref):
    _fill_ext(sh_ref.at[0], vprev_ref, v_ref, vnext_ref, n_tiles)
    shifted_rows = TM + 2 * BF16_ROWS - SUBLANES
    for r in range(1, SUBLANES):
        sh_ref[r, 0:shifted_rows, :] = sh_ref[0, r:r + shifted_rows, :]
    half = CONF_KERNEL // 2

    def row_block(i, carry):
        base = pl.multiple_of(i * CM_ROWS, CM_ROWS)
        for c0 in range(0, D_CONF, CM_COLS):
            cols = pl.ds(c0, CM_COLS)
            accs = [cb_ref[:, cols] for _ in range(CM_ROWS // SUBLANES)]
            for k in range(CONF_KERNEL):
                q, r = divmod(BF16_ROWS - half + k, SUBLANES)
                w8 = cw_ref[k, :, cols]
                for g in range(CM_ROWS // SUBLANES):
                    accs[g] = accs[g] + w8 * sh_ref[r, pl.ds(base + (q + g) * SUBLANES, SUBLANES), cols]
            for g in range(CM_ROWS // SUBLANES):
                cv_ref[pl.ds(base + g * SUBLANES, SUBLANES), cols] = accs[g]
        return carry

    lax.fori_loop(0, TM // CM_ROWS, row_block, 0)

    cv = cv_ref[...]
    cc = cv - jnp.mean(cv, axis=-1, keepdims=True)
    ln = cc * lax.rsqrt(jnp.mean(cc * cc, axis=-1, keepdims=True) + EPS) * lg_ref[...] + lb_ref[...]
    vv = (ln * _sigmoid(ln)).astype(BF16)

    yz = (yf_ref[0].astype(F32) + yb_ref[0].astype(F32)) * zs_ref[0].astype(F32)
    y_ssd = yz * lax.rsqrt(jnp.mean(yz * yz, axis=-1, keepdims=True) + EPS) * ng_ref[...]
    m = _dot(y_ssd.astype(BF16), w1_ref[...]) + _dot(vv, w2_ref[...])
    m = m * lax.rsqrt(jnp.mean(m * m, axis=-1, keepdims=True) + EPS) * pg_ref[...]
    xn = _stream_tile(head_ref, body_ref) + gate_ref[0, 0] * m
    xo_ref[0] = xn
    u = xn * lax.rsqrt(jnp.mean(xn * xn, axis=-1, keepdims=True) + EPS) * scale_ref[0, 0] + shift_ref[0, 0]
    ub = u.astype(BF16)
    u_ref[0] = ub
    lo_ref[0] = _dot(ub, rw_ref[...])


def _mix_out(yf, yb, zs, v, head, body, body_tile0, ng, cw, cb, lg, lb, w1, w2, pg, gate, scale, shift, rw):
    b, s, _ = zs.shape
    d = head.shape[-1]
    n_tiles = s // TM
    head_spec, body_spec = _stream_specs(body_tile0, d)
    prev, nxt = _halo_specs(D_CONF, n_tiles)
    full = lambda a: pl.BlockSpec(a.shape, lambda bi, t: (0,) * a.ndim)
    mod = pl.BlockSpec((1, 1, 1, d), lambda bi, t: (bi, _seg(t), 0, 0))
    tile = lambda width: pl.BlockSpec((1, TM, width), lambda bi, t: (bi, t, 0))
    return pl.pallas_call(
        functools.partial(_mix_out_kernel, n_tiles),
        grid=(b, n_tiles),
        in_specs=[tile(D_SSD), tile(D_SSD), tile(D_SSD), prev, tile(D_CONF), nxt, head_spec, body_spec,
                  full(ng), full(cw), full(cb), full(lg), full(lb), full(w1), full(w2), full(pg),
                  mod, mod, mod, full(rw)],
        out_specs=[tile(d), tile(d), tile(LANES)],
        out_shape=[jax.ShapeDtypeStruct((b, s, d), F32),
                   jax.ShapeDtypeStruct((b, s, d), BF16),
                   jax.ShapeDtypeStruct((b, s, LANES), F32)],
        scratch_shapes=[pltpu.VMEM((SUBLANES, TM + 2 * BF16_ROWS, D_CONF), F32),
                        pltpu.VMEM((TM, D_CONF), F32)],
        compiler_params=_params(("parallel", "parallel")),
        name="mix_out",
    )(yf, yb, zs, v, v, v, head, body, ng, cw, cb, lg, lb, w1, w2, pg, gate, scale, shift, rw)


FFN_COLS = 256


FFN_MAX_ROWS = 1024


def _ffn_kernel(sel_ref, wg_ref, wu_ref, wd_ref, o_ref):
    nb, _, rows, d = sel_ref.shape
    s = sel_ref[:, 0].reshape(nb * rows, d)
    f = wg_ref.shape[-1]
    out = jnp.zeros((nb * rows, d), F32)
    for c0 in range(0, f, FFN_COLS):
        c1 = min(c0 + FFN_COLS, f)
        hg = _dot(s, wg_ref[0, 0, :, c0:c1].astype(BF16))
        hu = _dot(s, wu_ref[0, 0, :, c0:c1].astype(BF16))
        h = (hg * _sigmoid(hg) * hu).astype(BF16)
        out = out + _dot(h, wd_ref[0, 0, c0:c1, :].astype(BF16))
    o_ref[:, 0] = out.astype(BF16).reshape(nb, rows, d)


def _expert_ffn(sel, layer, wg, wu, wd):
    b, e, rows, d = sel.shape
    f = wg.shape[-1]
    nb = max(k for k in range(1, b + 1) if b % k == 0 and k * rows <= max(rows, FFN_MAX_ROWS))
    return pl.pallas_call(
        _ffn_kernel,
        grid=(e, b // nb),
        in_specs=[pl.BlockSpec((nb, 1, rows, d), lambda ei, bi: (bi, ei, 0, 0)),
                  pl.BlockSpec((1, 1, d, f), lambda ei, bi: (layer, ei, 0, 0)),
                  pl.BlockSpec((1, 1, d, f), lambda ei, bi: (layer, ei, 0, 0)),
                  pl.BlockSpec((1, 1, f, d), lambda ei, bi: (layer, ei, 0, 0))],
        out_specs=pl.BlockSpec((nb, 1, rows, d), lambda ei, bi: (bi, ei, 0, 0)),
        out_shape=jax.ShapeDtypeStruct((b, e, rows, d), BF16),
        compiler_params=_params(("parallel", "arbitrary")),
        name="expert_ffn",
    )(sel, wg, wu, wd)


RT = 128
BISECT_BITS = 31


def _route_kernel(cap, row0, lg_ref, post_ref, posn_ref, gate_ref, starts_ref, aff_ref, afft_ref):
    n = lg_ref.shape[1] - row0
    nt = n // RT
    lane_n = lax.broadcasted_iota(jnp.int32, (RT, LANES), 1)
    real_n = lane_n < N_EXPERTS

    for i in range(nt):
        rows = slice(i * RT, (i + 1) * RT)
        lg = jnp.where(real_n, lg_ref[0, row0 + i * RT:row0 + (i + 1) * RT, :], -1e30)
        ex = jnp.exp(lg - jnp.max(lg, axis=-1, keepdims=True))
        aff = ex / jnp.sum(ex, axis=-1, keepdims=True)
        aff_ref[rows, :] = aff
        afft_ref[:, rows] = aff.T[0:N_EXPERTS, :]

    bits = lax.bitcast_convert_type(afft_ref[...], jnp.int32)

    def bisect(i, prefix):
        cand = prefix | jnp.left_shift(jnp.int32(1), BISECT_BITS - 1 - i)
        cnt = jnp.sum(jnp.where(bits >= cand, 1.0, 0.0), axis=1, keepdims=True)
        return jnp.where(cnt >= cap, cand, prefix)

    thr = lax.fori_loop(0, BISECT_BITS, bisect, jnp.zeros((N_EXPERTS, 1), jnp.int32))
    n_gt = jnp.sum(jnp.where(bits > thr, 1.0, 0.0), axis=1, keepdims=True)
    need = cap - n_gt

    r_i = lax.broadcasted_iota(jnp.int32, (RT, RT), 0)
    c_i = lax.broadcasted_iota(jnp.int32, (RT, RT), 1)
    upper = jnp.where(r_i <= c_i, 1.0, 0.0).astype(BF16)
    lower = jnp.where(c_i <= r_i, 1.0, 0.0).astype(BF16)
    lane_s = lax.broadcasted_iota(jnp.int32, (N_EXPERTS, LANES), 1)
    carry_eq = jnp.zeros((N_EXPERTS, 1), F32)
    carry_sel = jnp.zeros((N_EXPERTS, 1), F32)
    starts = jnp.zeros((N_EXPERTS, LANES), F32)
    for i in range(nt):
        cols = slice(i * RT, (i + 1) * RT)
        b_t = bits[:, cols]
        eq = jnp.where(b_t == thr, 1.0, 0.0)
        rank = _dot(eq.astype(BF16), upper) + carry_eq
        sel = jnp.where(jnp.logical_or(b_t > thr, jnp.logical_and(b_t == thr, rank <= need)), 1.0, 0.0)
        pos = _dot(sel.astype(BF16), upper) + carry_sel - sel
        post_ref[0, :, cols] = jnp.where(sel > 0.0, pos, -1.0).astype(jnp.int32)
        starts = jnp.where(lane_s == i, carry_sel, starts)
        carry_eq = carry_eq + jnp.sum(eq, axis=1, keepdims=True)
        carry_sel = carry_sel + jnp.sum(sel, axis=1, keepdims=True)
    starts_ref[0] = starts.astype(jnp.int32)

    sub_s = lax.broadcasted_iota(jnp.int32, (N_EXPERTS, LANES), 0)
    eye = sub_s == lane_s
    thr_row = jnp.sum(jnp.where(eye, lax.bitcast_convert_type(thr, F32), 0.0), axis=0, keepdims=True)
    need_row = jnp.sum(jnp.where(eye, need, 0.0), axis=0, keepdims=True)
    carry_eq = jnp.zeros((1, LANES), F32)
    carry_sel = jnp.zeros((1, LANES), F32)
    for i in range(nt):
        rows = slice(i * RT, (i + 1) * RT)
        aff = aff_ref[rows, :]
        eq_b = jnp.logical_and(aff == thr_row, real_n)
        eq = jnp.where(eq_b, 1.0, 0.0)
        rank = _dot(lower, eq.astype(BF16)) + carry_eq
        sel_b = jnp.logical_and(real_n, jnp.logical_or(aff > thr_row, jnp.logical_and(eq_b, rank <= need_row)))
        sel = jnp.where(sel_b, 1.0, 0.0)
        pos = _dot(lower, sel.astype(BF16)) + carry_sel - sel
        posn_ref[0, rows, :] = jnp.where(sel_b, pos, -1.0).astype(jnp.int32)
        gate_ref[0, rows, :] = jnp.where(sel_b, aff, 0.0)
        carry_eq = carry_eq + jnp.sum(eq, axis=0, keepdims=True)
        carry_sel = carry_sel + jnp.sum(sel, axis=0, keepdims=True)


def _route(logits, row0, n, cap):
    b = logits.shape[0]
    return pl.pallas_call(
        functools.partial(_route_kernel, cap, row0),
        grid=(b,),
        in_specs=[pl.BlockSpec((1, row0 + n, LANES), lambda bi: (bi, 0, 0))],
        out_specs=[pl.BlockSpec((1, N_EXPERTS, n), lambda bi: (bi, 0, 0)),
                   pl.BlockSpec((1, n, LANES), lambda bi: (bi, 0, 0)),
                   pl.BlockSpec((1, n, LANES), lambda bi: (bi, 0, 0)),
                   pl.BlockSpec((1, N_EXPERTS, LANES), lambda bi: (bi, 0, 0))],
        out_shape=[jax.ShapeDtypeStruct((b, N_EXPERTS, n), jnp.int32),
                   jax.ShapeDtypeStruct((b, n, LANES), jnp.int32),
                   jax.ShapeDtypeStruct((b, n, LANES), F32),
                   jax.ShapeDtypeStruct((b, N_EXPERTS, LANES), jnp.int32)],
        scratch_shapes=[pltpu.VMEM((n, LANES), F32), pltpu.VMEM((N_EXPERTS, n), F32)],
        compiler_params=_params(("parallel",)),
        name="route",
    )(logits)


GT = 256
MIN_ROWS = 128


def _window(start, align, width, rows):
    if width == rows:
        return 0
    a = jnp.minimum((start // align) * align, rows - width)
    return pl.multiple_of(a, align)


def _gather_kernel(nt_r, row0, starts_ref, post_ref, u_ref, sel_ref, acc_ref):
    bi, ei = pl.program_id(0), pl.program_id(1)
    rows = acc_ref.shape[0]
    n = u_ref.shape[1] - row0
    gt = min(GT, n)
    width = min(gt + BF16_ROWS, rows)
    acc_ref[...] = jnp.zeros_like(acc_ref)
    j = lax.broadcasted_iota(jnp.int32, (width, gt), 0)
    for i in range(n // gt):
        start = starts_ref[(bi * N_EXPERTS + ei) * nt_r + i * (gt // RT)]
        a = _window(start, BF16_ROWS, width, rows)
        p = post_ref[0, 0, :, i * gt:(i + 1) * gt]
        onehot = jnp.where(p - a == j, 1.0, 0.0).astype(BF16)
        acc_ref[pl.ds(a, width), :] += _dot(onehot, u_ref[0, row0 + i * gt:row0 + (i + 1) * gt, :])
    sel_ref[0, 0] = acc_ref[...].astype(BF16)


def _gather(starts, post, u, row0, rows):
    b, _, d = u.shape
    n = post.shape[-1]
    nt_r = n // RT
    post4 = post.reshape(b, N_EXPERTS, 1, n)
    return pl.pallas_call(
        functools.partial(_gather_kernel, nt_r, row0),
        grid_spec=pltpu.PrefetchScalarGridSpec(
            num_scalar_prefetch=1,
            grid=(b, N_EXPERTS),
            in_specs=[pl.BlockSpec((1, 1, 1, n), lambda bi, ei, st: (bi, ei, 0, 0)),
                      pl.BlockSpec((1, row0 + n, d), lambda bi, ei, st: (bi, 0, 0))],
            out_specs=pl.BlockSpec((1, 1, rows, d), lambda bi, ei, st: (bi, ei, 0, 0)),
            scratch_shapes=[pltpu.VMEM((rows, d), F32)]),
        out_shape=jax.ShapeDtypeStruct((b, N_EXPERTS, rows, d), BF16),
        compiler_params=_params(("parallel", "parallel")),
        name="moe_gather",
    )(starts, post4, u)


def _scatter_kernel(nt_r, starts_ref, out_ref, posn_ref, gate_ref, x_ref, pg_ref, g2_ref, xo_ref):
    bi, ti = pl.program_id(0), pl.program_id(1)
    rows = out_ref.shape[2]
    width = min(RT + LANES, rows)
    j = lax.broadcasted_iota(jnp.int32, (RT, width), 1)
    for sub in range(TM // RT):
        r0 = sub * RT
        y = jnp.zeros((RT, D_MODEL), F32)
        for e in range(N_EXPERTS):
            start = starts_ref[(bi * N_EXPERTS + e) * nt_r + ti * (TM // RT) + sub]
            a = _window(start, LANES, width, rows)
            p = posn_ref[0, r0:r0 + RT, e:e + 1]
            onehot = jnp.where(p - a == j, 1.0, 0.0).astype(BF16)
            y = y + gate_ref[0, r0:r0 + RT, e:e + 1] * _dot(onehot, out_ref[0, e, pl.ds(a, width), :])
        y = y * lax.rsqrt(jnp.mean(y * y, axis=-1, keepdims=True) + EPS) * pg_ref[...]
        xo_ref[0, r0:r0 + RT, :] = x_ref[0, r0:r0 + RT, :] + g2_ref[0, 0] * y


def _scatter(starts, out, posn, gate, xc, pg, g2, seg, in_tile0, out_tile0, out_rows, alias):
    b, n, _ = posn.shape
    d = xc.shape[-1]
    rows = out.shape[2]
    nt_r = n // RT
    return pl.pallas_call(
        functools.partial(_scatter_kernel, nt_r),
        grid_spec=pltpu.PrefetchScalarGridSpec(
            num_scalar_prefetch=1,
            grid=(b, n // TM),
            in_specs=[pl.BlockSpec((1, N_EXPERTS, rows, d), lambda bi, ti, st: (bi, 0, 0, 0)),
                      pl.BlockSpec((1, TM, LANES), lambda bi, ti, st: (bi, ti, 0)),
                      pl.BlockSpec((1, TM, LANES), lambda bi, ti, st: (bi, ti, 0)),
                      pl.BlockSpec((1, TM, d), lambda bi, ti, st: (bi, ti + in_tile0, 0)),
                      pl.BlockSpec((1, d), lambda bi, ti, st: (0, 0)),
                      pl.BlockSpec((1, 1, 1, d), lambda bi, ti, st: (bi, seg, 0, 0))],
            out_specs=pl.BlockSpec((1, TM, d), lambda bi, ti, st: (bi, ti + out_tile0, 0))),
        out_shape=jax.ShapeDtypeStruct((b, out_rows, d), F32),
        input_output_aliases={4: 0} if alias else {},
        compiler_params=_params(("parallel", "parallel")),
        name="moe_scatter",
    )(starts, out, posn, gate, xc, pg, g2)


def _moe_segment(xc, u, logits, row0, n, layer, wg, wu, wd, pg, g2, seg, out_tile0, out_rows, alias):
    cap = max(1, CAPACITY_FACTOR * n // N_EXPERTS)
    rows = max(cap, MIN_ROWS)
    post, posn, gate, starts = _route(logits, row0, n, cap)
    starts = starts[:, :, :n // RT].reshape(-1)
    sel = _gather(starts, post, u, row0, rows)
    out = _expert_ffn(sel, layer, wg, wu, wd)
    return _scatter(starts, out, posn, gate, xc, pg, g2, seg, row0 // TM, out_tile0, out_rows, alias)


def _pad_cols(w, n):
    return jnp.pad(w, ((0, 0), (0, n - w.shape[1])))


def kernel(x, c, ctx, c_ctx, ada_w, ada_b, mix_pre_g, w_in, conv_w, conv_b, dt_bias, a_log, d_skip,
           ssd_norm_g, cm_dw_w, cm_dw_b, cm_ln_g, cm_ln_b, w_out, mix_post_g, moe_pre_g, router_w,
           exp_w_gate, exp_w_up, exp_w_down, moe_post_g):
    b = x.shape[0]
    s_tot = CTX_LEN + x.shape[1]
    head, body, body_tile0 = ctx, x, 0
    row = lambda v: v.reshape(1, -1)
    sub8 = lambda v: jnp.broadcast_to(v, v.shape[:-2] + (SUBLANES, v.shape[-1]))

    li = lax.broadcasted_iota(jnp.int32, (CHUNK, CHUNK), 0)
    si = lax.broadcasted_iota(jnp.int32, (CHUNK, CHUNK), 1)
    tri = jnp.stack([(si <= li), (si >= li)]).astype(F32)

    cond = jnp.zeros((8, D_MODEL), F32).at[:b].set(c).at[b].set(c_ctx)
    cond = cond * _sigmoid(cond)

    for i in range(DEPTH):
        last = i == DEPTH - 1
        mods = _matmul(cond, ada_w[i]) + ada_b[i]
        mods = mods.reshape(8, 6, D_MODEL)
        both = lambda j: jnp.stack([jnp.broadcast_to(mods[b, j], (b, D_MODEL)), mods[:b, j]], axis=1)[:, :, None, :]
        sh1, sc1, g1, sh2, sc2, g2 = [both(j) for j in range(6)]

        wi = w_in[i].astype(BF16)
        wxbc = wi[:, :XBC_DIM]
        wdtf = _pad_cols(wi[:, XBC_DIM:XBC_DIM + N_HEADS], LANES)
        wdtb = _pad_cols(wi[:, XBC_DIM + N_HEADS:SSD_IN_COLS], LANES)
        wz = wi[:, SSD_IN_COLS:Z_END]
        wval = wi[:, Z_END:Z_END + D_CONF]
        wgate = wi[:, Z_END + D_CONF:]
        dtbf = _pad_cols(row(dt_bias[i, 0]), LANES)
        dtbb = _pad_cols(row(dt_bias[i, 1]), LANES)
        xbc, dt, zs, v = _mix_in(head, body, body_tile0, s_tot, mix_pre_g[i] * (1.0 + sc1), sh1, wxbc, wdtf, wdtb, wz, wval, wgate,
                                 dtbf, dtbb)

        xbcs = _conv_silu(xbc, conv_w[i], row(conv_b[i]))

        a_dir = _pad_cols(-jnp.exp(a_log[i]), LANES)[:, None, :]
        yf, yb = _ssd_scan(xbcs, dt, tri, a_dir, row(jnp.repeat(d_skip[i], HEAD_DIM)))

        wo = w_out[i].astype(BF16)
        rw = _pad_cols(router_w[i].astype(BF16), LANES)
        xc, u, logits = _mix_out(yf, yb, zs, v, head, body, body_tile0, row(ssd_norm_g[i]), sub8(cm_dw_w[i][:, None, :]), sub8(row(cm_dw_b[i])),
                                 row(cm_ln_g[i]), row(cm_ln_b[i]), wo[:D_SSD], wo[D_SSD:],
                                 row(mix_post_g[i]), g1, moe_pre_g[i] * (1.0 + sc2), sh2, rw)

        wg, wu, wd = exp_w_gate, exp_w_up, exp_w_down
        pg = row(moe_post_g[i])
        n_lat = s_tot - CTX_LEN
        if not last:
            xc = _moe_segment(xc, u, logits, CTX_LEN, n_lat, i, wg, wu, wd, pg, g2, 1, CTX_LEN // TM, s_tot, True)
            xc = _moe_segment(xc, u, logits, 0, CTX_LEN, i, wg, wu, wd, pg, g2, 0, 0, s_tot, True)
            head, body, body_tile0 = xc, xc, CTX_LEN // TM
        else:
            xc = _moe_segment(xc, u, logits, CTX_LEN, n_lat, i, wg, wu, wd, pg, g2, 1, 0, n_lat, False)
    return xc
```
